```python
import jax, jax.numpy as jnp
from jax import lax
import numpy as np

D_MODEL = 4096
BATCH = 4
SEQ = 2048
DEPTH = 2
DEC_BATCH = 32
DEC_SEQ = 4
PAST_LEN = 16384
PAGE_SIZE = 128

HEAD_DIM_A = 64
N_HEADS_A = (D_MODEL // 2) // HEAD_DIM_A
N_KV_A = N_HEADS_A // 8
GROUP_A = N_HEADS_A // N_KV_A
WINDOW = 128
BLOCK = 128
ROPE_THETA = 10000.0
D_CONV = D_MODEL // 2
CONV_W = 31
HEAD_C = 64
N_HEADS_C = D_MODEL // HEAD_C
R_DECAY = max(32, round(1.8 * D_MODEL ** 0.5 / 32) * 32)
R_A = max(32, round(1.8 * D_MODEL ** 0.5 / 32) * 32)
R_G = max(32, round(0.6 * D_MODEL ** 0.8 / 32) * 32)
D_FF = 256 * ((8 * D_MODEL // 3 + 255) // 256)
N_EXPERTS = 8
TOP_K = 2
D_FF_EXPERT = D_MODEL * 7 // 2
ALPHA = (2 * DEPTH) ** 0.25
BETA = (8 * DEPTH) ** -0.25
N_EVEN = (DEPTH + 1) // 2
N_ODD = DEPTH // 2
D_Q_A = N_HEADS_A * HEAD_DIM_A
D_KV_A = N_KV_A * HEAD_DIM_A
D_IN = D_Q_A + 2 * D_KV_A + 2 * D_CONV
D_MIX = D_Q_A + D_CONV
LN_EPS = 1e-5
GN_EPS = 64e-5

kernel_name = 'hybrid_swa_conformer_rwkv7_deepnorm_step'


def layer_norm(x, g, b, eps=LN_EPS):
    xf = x.astype(jnp.float32)
    m = jnp.mean(xf, axis=-1, keepdims=True)
    var = jnp.mean(jnp.square(xf - m), axis=-1, keepdims=True)
    return ((xf - m) * lax.rsqrt(var + eps) * g + b).astype(x.dtype)


def rope(x, pos):
    half = HEAD_DIM_A // 2
    inv = jnp.power(ROPE_THETA, -jnp.arange(half, dtype=jnp.float32) / half)
    ang = pos.astype(jnp.float32)[:, None] * inv[None, :]
    cos, sin = jnp.cos(ang)[:, None, :], jnp.sin(ang)[:, None, :]
    xf = x.astype(jnp.float32)
    x1, x2 = xf[..., :half], xf[..., half:]
    return jnp.concatenate([x1 * cos - x2 * sin, x2 * cos + x1 * sin], axis=-1).astype(x.dtype)


def sink_attention(q, k, v, mask, sinks):
    b, n, nq = q.shape[:3]
    qg = q.reshape(b, n, nq, N_KV_A, GROUP_A, HEAD_DIM_A)
    s = jnp.einsum('bnqkgd,bnskd->bnkgqs', qg, k, preferred_element_type=jnp.float32) * (HEAD_DIM_A ** -0.5)
    s = jnp.where(mask[None, :, None, None], s, -jnp.inf)
    sink = sinks.astype(jnp.float32).reshape(N_KV_A, GROUP_A)[None, None, :, :, None, None]
    m = jnp.maximum(jnp.max(s, axis=-1, keepdims=True), sink)
    p = jnp.exp(s - m)
    den = jnp.sum(p, axis=-1, keepdims=True) + jnp.exp(sink - m)
    o = jnp.einsum('bnkgqs,bnskd->bnqkgd', p / den, v.astype(jnp.float32))
    return o.reshape(b, n, nq, D_Q_A).astype(q.dtype)


def swa_prompt(q, k, v, sinks):
    b, t = q.shape[:2]
    nb = t // BLOCK
    qb = q.reshape(b, nb, BLOCK, N_HEADS_A, HEAD_DIM_A)

    def band(z):
        zp = jnp.pad(z, ((0, 0), (BLOCK, 0), (0, 0), (0, 0))).reshape(b, nb + 1, BLOCK, N_KV_A, HEAD_DIM_A)
        return jnp.concatenate([zp[:, :-1], zp[:, 1:]], axis=2)

    qpos = jnp.arange(t, dtype=jnp.int32).reshape(nb, BLOCK)
    kpos = qpos[:, :1] - BLOCK + jnp.arange(2 * BLOCK, dtype=jnp.int32)[None, :]
    diff = qpos[:, :, None] - kpos[:, None, :]
    mask = (kpos[:, None, :] >= 0) & (diff >= 0) & (diff <= WINDOW)
    return sink_attention(qb, band(k), band(v), mask, sinks).reshape(b, t, D_Q_A)


def swa_sample(q, k, v, cache_k, cache_v, sinks):
    s = q.shape[1]
    kc = jnp.concatenate([cache_k.astype(k.dtype), k], axis=1)
    vc = jnp.concatenate([cache_v.astype(v.dtype), v], axis=1)
    qpos = PAST_LEN + jnp.arange(s, dtype=jnp.int32)
    kpos = PAST_LEN - WINDOW + jnp.arange(WINDOW + s, dtype=jnp.int32)
    diff = qpos[:, None] - kpos[None, :]
    mask = ((diff >= 0) & (diff <= WINDOW))[None]
    o = sink_attention(q[:, None], kc[:, None], vc[:, None], mask, sinks)[:, 0]
    return o, kc[:, -WINDOW:], vc[:, -WINDOW:]


def conv_branch(glu, hist, conv_w, conv_b, ln_g, ln_b):
    a, gate = jnp.split(glu, 2, axis=-1)
    u = a * jax.nn.sigmoid(gate)
    if hist is None:
        hist = jnp.zeros((u.shape[0], CONV_W - 1, D_CONV), u.dtype)
    up = jnp.concatenate([hist.astype(u.dtype), u], axis=1)
    y = lax.conv_general_dilated(up, conv_w.astype(u.dtype)[:, None, :], window_strides=(1,), padding='VALID',
                                 dimension_numbers=('NWC', 'WIO', 'NWC'), feature_group_count=D_CONV) + conv_b
    return jax.nn.silu(layer_norm(y, ln_g, ln_b)), up[:, -(CONV_W - 1):]


def even_mixer(x, pos, conv_hist, kv_cache, w_in, b_in, sinks, conv_w, conv_b, conv_g, conv_bn, w_out, b_out):
    b, t, _ = x.shape
    h = x @ w_in + b_in
    q, k, v, glu = jnp.split(h, [D_Q_A, D_Q_A + D_KV_A, D_Q_A + 2 * D_KV_A], axis=-1)
    q = rope(q.reshape(b, t, N_HEADS_A, HEAD_DIM_A), pos)
    k = rope(k.reshape(b, t, N_KV_A, HEAD_DIM_A), pos)
    v = v.reshape(b, t, N_KV_A, HEAD_DIM_A)
    if kv_cache is None:
        att = swa_prompt(q, k, v, sinks)
        new_k, new_v = k[:, -WINDOW:], v[:, -WINDOW:]
    else:
        att, new_k, new_v = swa_sample(q, k, v, kv_cache[0], kv_cache[1], sinks)
    conv, new_conv = conv_branch(glu, conv_hist, conv_w, conv_b, conv_g, conv_bn)
    y = jnp.concatenate([att, conv], axis=-1) @ w_out + b_out
    return y, new_k, new_v, new_conv


def rwkv7_mixer(x, shift, wkv0, mu, w0, w1, w2, a0, a1, a2, g1, g2, k_k, k_a, r_k, w_r, w_k, w_v, w_o, lnx_g, lnx_b):
    b, t, d = x.shape
    prev = jnp.concatenate([shift[:, None].astype(x.dtype), x[:, :-1]], axis=1)
    xx = prev - x
    xr, xw, xk, xv, xa, xg = [x + xx * mu[i] for i in range(6)]
    r = xr @ w_r
    w = -jax.nn.softplus(-(w0 + jnp.tanh(xw @ w1) @ w2)) - 0.5
    k = xk @ w_k
    v = xv @ w_v
    a = jax.nn.sigmoid(a0 + (xa @ a1) @ a2)
    g = jax.nn.sigmoid(xg @ g1) @ g2

    def heads(z):
        return z.reshape(b, t, N_HEADS_C, HEAD_C).astype(jnp.float32)

    kk = heads(k * k_k)
    kk = kk * lax.rsqrt(jnp.maximum(jnp.sum(kk * kk, axis=-1, keepdims=True), 1e-24))
    k = k * (1 + (a - 1) * k_a)
    r, w, k, v, a = heads(r), heads(w), heads(k), heads(v), heads(a)
    decay = jnp.exp(-jnp.exp(w))

    def step(S, inp):
        r_t, d_t, k_t, v_t, kk_t, a_t = inp
        sa = jnp.einsum('bhvk,bhk->bhv', S, -kk_t)
        S = S * d_t[:, :, None, :] + sa[..., None] * (kk_t * a_t)[:, :, None, :] + v_t[..., None] * k_t[:, :, None, :]
        return S, jnp.einsum('bhvk,bhk->bhv', S, r_t)

    seq = (jnp.moveaxis(r, 1, 0), jnp.moveaxis(decay, 1, 0), jnp.moveaxis(k, 1, 0),
           jnp.moveaxis(v, 1, 0), jnp.moveaxis(kk, 1, 0), jnp.moveaxis(a, 1, 0))
    S, o = lax.scan(step, wkv0.astype(jnp.float32), seq)
    o = jnp.moveaxis(o, 0, 1)
    o_mean = jnp.mean(o, axis=-1, keepdims=True)
    o_var = jnp.mean(jnp.square(o - o_mean), axis=-1, keepdims=True)
    o = ((o - o_mean) * lax.rsqrt(o_var + GN_EPS)).reshape(b, t, d) * lnx_g + lnx_b
    o = o + (jnp.sum(r * k * r_k, axis=-1, keepdims=True) * v).reshape(b, t, d)
    y = (o * g).astype(x.dtype) @ w_o
    return y, x[:, -1], S


def swiglu(x, wg, wu, wd):
    return (jax.nn.silu(x @ wg) * (x @ wu)) @ wd


def moe_swiglu(x, w_router, wg, wu, wd):
    logits = jnp.einsum('td,de->te', x, w_router, preferred_element_type=jnp.float32)
    top_v, top_i = lax.top_k(logits, TOP_K)
    gates = jax.nn.softmax(top_v, axis=-1)
    dense_gate = jnp.sum(jax.nn.one_hot(top_i, N_EXPERTS, dtype=jnp.float32) * gates[..., None], axis=1)
    y = jnp.zeros(x.shape, jnp.float32)
    for e in range(N_EXPERTS):
        y = y + dense_gate[:, e:e + 1] * swiglu(x, wg[e], wu[e], wd[e]).astype(jnp.float32)
    return y.astype(x.dtype)


def setup_inputs(seed: int = 0) -> dict:
    key = jax.random.key(seed)
    ks = iter(jax.random.split(key, 48))
    d = D_MODEL

    def nrm(shape, scale):
        return scale * jax.random.normal(next(ks), shape, jnp.float32)

    def uni(shape, lo, hi):
        return jax.random.uniform(next(ks), shape, jnp.float32, lo, hi)

    col_scale = jnp.ones((D_IN,), jnp.float32).at[D_Q_A + D_KV_A:D_Q_A + 2 * D_KV_A].set(BETA)
    return {
        'x_prompt': nrm((BATCH, SEQ, d), 1.0),
        'x_sample': nrm((DEC_BATCH, DEC_SEQ, d), 1.0),
        'cache_swa_k': nrm((N_EVEN, DEC_BATCH, WINDOW, N_KV_A, HEAD_DIM_A), 1.0),
        'cache_swa_v': nrm((N_EVEN, DEC_BATCH, WINDOW, N_KV_A, HEAD_DIM_A), 1.0),
        'state_conv': nrm((N_EVEN, DEC_BATCH, CONV_W - 1, D_CONV), 0.5),
        'state_shift': nrm((N_ODD, DEC_BATCH, d), 1.0),
        'state_wkv': nrm((N_ODD, DEC_BATCH, N_HEADS_C, HEAD_C, HEAD_C), 0.3),
        'ln_g': 1.0 + nrm((DEPTH, 2, d), 0.02),
        'ln_b': nrm((DEPTH, 2, d), 0.02),
        'w_in': nrm((N_EVEN, d, D_IN), d ** -0.5) * col_scale,
        'b_in': nrm((N_EVEN, D_IN), 0.02),
        'sinks': nrm((N_EVEN, N_HEADS_A), 1.0),
        'conv_w': nrm((N_EVEN, CONV_W, D_CONV), CONV_W ** -0.5),
        'conv_b': nrm((N_EVEN, D_CONV), 0.02),
        'conv_ln_g': 1.0 + nrm((N_EVEN, D_CONV), 0.02),
        'conv_ln_b': nrm((N_EVEN, D_CONV), 0.02),
        'w_out': nrm((N_EVEN, D_MIX, d), BETA * D_MIX ** -0.5),
        'b_out': nrm((N_EVEN, d), 0.02),
        'ffn_w_gate': nrm((N_EVEN, d, D_FF), d ** -0.5),
        'ffn_w_up': nrm((N_EVEN, d, D_FF), d ** -0.5),
        'ffn_w_down': nrm((N_EVEN, D_FF, d), BETA * D_FF ** -0.5),
        'mu': uni((N_ODD, 6, d), 0.0, 1.0),
        'w0': uni((N_ODD, d), -4.0, 1.0),
        'w1': nrm((N_ODD, d, R_DECAY), d ** -0.5),
        'w2': nrm((N_ODD, R_DECAY, d), 0.5 * R_DECAY ** -0.5),
        'a0': nrm((N_ODD, d), 0.1),
        'a1': nrm((N_ODD, d, R_A), d ** -0.5),
        'a2': nrm((N_ODD, R_A, d), 0.5 * R_A ** -0.5),
        'g1': nrm((N_ODD, d, R_G), d ** -0.5),
        'g2': nrm((N_ODD, R_G, d), R_G ** -0.5),
        'k_k': uni((N_ODD, d), 0.7, 1.0),
        'k_a': uni((N_ODD, d), 0.8, 1.2),
        'r_k': nrm((N_ODD, N_HEADS_C, HEAD_C), 0.1),
        'w_r': nrm((N_ODD, d, d), d ** -0.5),
        'w_k': nrm((N_ODD, d, d), d ** -0.5),
        'w_v': nrm((N_ODD, d, d), BETA * d ** -0.5),
        'w_o': nrm((N_ODD, d, d), BETA * d ** -0.5),
        'lnx_g': 1.0 + nrm((N_ODD, d), 0.02),
        'lnx_b': nrm((N_ODD, d), 0.02),
        'w_router': nrm((N_ODD, d, N_EXPERTS), d ** -0.5),
        'moe_w_gate': nrm((N_ODD, N_EXPERTS, d, D_FF_EXPERT), d ** -0.5),
        'moe_w_up': nrm((N_ODD, N_EXPERTS, d, D_FF_EXPERT), d ** -0.5),
        'moe_w_down': nrm((N_ODD, N_EXPERTS, D_FF_EXPERT, d), BETA * D_FF_EXPERT ** -0.5),
    }


def reference(x_prompt, x_sample, cache_swa_k, cache_swa_v, state_conv, state_shift, state_wkv,
              ln_g, ln_b, w_in, b_in, sinks, conv_w, conv_b, conv_ln_g, conv_ln_b, w_out, b_out,
              ffn_w_gate, ffn_w_up, ffn_w_down, mu, w0, w1, w2, a0, a1, a2, g1, g2, k_k, k_a, r_k,
              w_r, w_k, w_v, w_o, lnx_g, lnx_b, w_router, moe_w_gate, moe_w_up, moe_w_down):
    xp, xs = x_prompt, x_sample
    bp, tp = xp.shape[:2]
    n_p = bp * tp
    pos_p = jnp.arange(tp, dtype=jnp.int32)
    pos_s = PAST_LEN + jnp.arange(xs.shape[1], dtype=jnp.int32)
    kp_l, vp_l, ks_l, vs_l, cp_l, cs_l = [], [], [], [], [], []
    shp_l, shs_l, wp_l, ws_l = [], [], [], []
    for layer in range(DEPTH):
        i = layer // 2
        if layer % 2 == 0:
            prm = (w_in[i], b_in[i], sinks[i], conv_w[i], conv_b[i], conv_ln_g[i], conv_ln_b[i], w_out[i], b_out[i])
            mp, kp, vp, cp = even_mixer(xp, pos_p, None, None, *prm)
            ms, ks_, vs, cs = even_mixer(xs, pos_s, state_conv[i], (cache_swa_k[i], cache_swa_v[i]), *prm)
            kp_l.append(kp); vp_l.append(vp); ks_l.append(ks_); vs_l.append(vs); cp_l.append(cp); cs_l.append(cs)
        else:
            prm = (mu[i], w0[i], w1[i], w2[i], a0[i], a1[i], a2[i], g1[i], g2[i], k_k[i], k_a[i], r_k[i],
                   w_r[i], w_k[i], w_v[i], w_o[i], lnx_g[i], lnx_b[i])
            shift0 = jnp.zeros((bp, D_MODEL), xp.dtype)
            wkv0 = jnp.zeros((bp, N_HEADS_C, HEAD_C, HEAD_C), jnp.float32)
            mp, shp, wkp = rwkv7_mixer(xp, shift0, wkv0, *prm)
            ms, shs, wks = rwkv7_mixer(xs, state_shift[i], state_wkv[i], *prm)
            shp_l.append(shp); shs_l.append(shs); wp_l.append(wkp); ws_l.append(wks)
        xp = layer_norm(ALPHA * xp + mp, ln_g[layer, 0], ln_b[layer, 0])
        xs = layer_norm(ALPHA * xs + ms, ln_g[layer, 0], ln_b[layer, 0])
        flat = jnp.concatenate([xp.reshape(-1, D_MODEL), xs.reshape(-1, D_MODEL)], axis=0)
        if layer % 2 == 0:
            f = swiglu(flat, ffn_w_gate[i], ffn_w_up[i], ffn_w_down[i])
        else:
            f = moe_swiglu(flat, w_router[i], moe_w_gate[i], moe_w_up[i], moe_w_down[i])
        flat = layer_norm(ALPHA * flat + f, ln_g[layer, 1], ln_b[layer, 1])
        xp = flat[:n_p].reshape(xp.shape)
        xs = flat[n_p:].reshape(xs.shape)
    swa_k_prompt = jnp.stack(kp_l)
    swa_v_prompt = jnp.stack(vp_l)
    swa_k_sample = jnp.stack(ks_l)
    swa_v_sample = jnp.stack(vs_l)
    conv_prompt = jnp.stack(cp_l)
    conv_sample = jnp.stack(cs_l)
    shift_prompt = jnp.stack(shp_l)
    shift_sample = jnp.stack(shs_l)
    wkv_prompt = jnp.stack(wp_l)
    wkv_sample = jnp.stack(ws_l)
    return (xp, xs, swa_k_prompt, swa_v_prompt, swa_k_sample, swa_v_sample, conv_prompt, conv_sample,
            shift_prompt, shift_sample, wkv_prompt, wkv_sample)
```

```python
import functools

import jax
import jax.numpy as jnp
from jax import lax
from jax.experimental import pallas as pl
from jax.experimental.pallas import tpu as pltpu

F32 = jnp.float32
BF16 = jnp.bfloat16

PAST_LEN = 16384
ROPE_THETA = 10000.0
LN_EPS = 1e-5
GN_EPS = 64e-5
TOP_K = 2
NEG_INF = float("-inf")

VMEM_LIMIT_BYTES = 58 * 1024 * 1024
LANES = 128
BF16_SUBLANES = 16


def _cparams(*sem):
    return pltpu.CompilerParams(dimension_semantics=sem, vmem_limit_bytes=VMEM_LIMIT_BYTES)


def _tile(n, target, mult):
    best = None
    for t in range(mult, min(n, target) + 1, mult):
        if n % t == 0:
            best = t
    return best if best is not None else n


def _mm_kernel(te_ref, tf_ref, tv_ref, *refs, n_w, has_bias, has_add):
    del te_ref
    x_ref = refs[0]
    w_refs = refs[1:1 + n_w]
    pos = 1 + n_w
    b_ref = a_ref = None
    if has_bias:
        b_ref = refs[pos]
        pos += 1
    if has_add:
        a_ref = refs[pos]
        pos += 1
    o_ref = refs[pos]
    wb_refs = refs[pos + 1:pos + 1 + n_w]
    t = pl.program_id(1)

    @pl.when(tf_ref[t] == 1)
    def _():
        for w_ref, wb_ref in zip(w_refs, wb_refs):
            wb_ref[...] = w_ref[0].astype(BF16)

    @pl.when(tv_ref[t] == 1)
    def _():
        x = x_ref[...]
        acc = jnp.dot(x, wb_refs[0][...], preferred_element_type=F32)
        if n_w == 2:
            up = jnp.dot(x, wb_refs[1][...], preferred_element_type=F32)
            acc = acc * jax.nn.sigmoid(acc) * up
        if has_bias:
            acc = acc + b_ref[...]
        if has_add:
            acc = acc + a_ref[...]
        o_ref[...] = acc.astype(o_ref.dtype)

    @pl.when(tv_ref[t] == 0)
    def _():
        o_ref[...] = jnp.zeros(o_ref.shape, o_ref.dtype)


def _mm(x, ws, *, tm, tn, nk=1, kk=0, bias=None, add=None, out_dtype=F32, tiles=None):
    m, k = x.shape
    n = ws[0].shape[-1]
    assert m % tm == 0 and n % tn == 0 and k % nk == 0
    tk = k // nk
    nt = m // tm
    if tiles is None:
        te = jnp.zeros((nt,), jnp.int32)
        tf = jnp.zeros((nt,), jnp.int32).at[0].set(1)
        tv = jnp.ones((nt,), jnp.int32)
    else:
        te, tf, tv = tiles
    n_w = len(ws)
    in_specs = [pl.BlockSpec((tm, tk), lambda j, t, te, tf, tv: (t, kk))]
    for _ in ws:
        in_specs.append(pl.BlockSpec((1, tk, tn), lambda j, t, te, tf, tv: (te[t], kk, j)))
    args = [x, *ws]
    if bias is not None:
        in_specs.append(pl.BlockSpec((1, tn), lambda j, t, te, tf, tv: (0, j)))
        args.append(bias.reshape(1, n))
    if add is not None:
        in_specs.append(pl.BlockSpec((tm, tn), lambda j, t, te, tf, tv: (t, j)))
        args.append(add)
    kern = functools.partial(_mm_kernel, n_w=n_w, has_bias=bias is not None, has_add=add is not None)
    return pl.pallas_call(
        kern,
        grid_spec=pltpu.PrefetchScalarGridSpec(
            num_scalar_prefetch=3,
            grid=(n // tn, nt),
            in_specs=in_specs,
            out_specs=pl.BlockSpec((tm, tn), lambda j, t, te, tf, tv: (t, j)),
            scratch_shapes=[pltpu.VMEM((tk, tn), BF16) for _ in ws],
        ),
        out_shape=jax.ShapeDtypeStruct((m, n), out_dtype),
        compiler_params=_cparams("arbitrary", "arbitrary"),
    )(te, tf, tv, *args)


def _dense(x, w, *, tm, tn, nk=1, bias=None, out_dtype=F32):
    out = None
    for kk in range(nk):
        last = kk == nk - 1
        out = _mm(x, [w[None]], tm=tm, tn=tn, nk=nk, kk=kk, bias=bias if kk == 0 else None, add=out,
                  out_dtype=out_dtype if last else F32)
    return out


def _ln_rows(z, g, b):
    mean = jnp.mean(z, axis=-1, keepdims=True)
    zc = z - mean
    var = jnp.mean(zc * zc, axis=-1, keepdims=True)
    return zc * lax.rsqrt(var + LN_EPS) * g + b


def _add_ln_kernel(x_ref, y_ref, g_ref, b_ref, o_ref, ob_ref, *, alpha):
    out = _ln_rows(alpha * x_ref[...] + y_ref[...], g_ref[...], b_ref[...])
    o_ref[...] = out
    ob_ref[...] = out.astype(BF16)


def _add_ln(x, y, g, b, alpha):
    m, d = x.shape
    tm = _tile(m, 256, BF16_SUBLANES)
    row = pl.BlockSpec((tm, d), lambda i: (i, 0))
    vec = pl.BlockSpec((1, d), lambda i: (0, 0))
    return pl.pallas_call(
        functools.partial(_add_ln_kernel, alpha=alpha),
        grid=(m // tm,),
        in_specs=[row, row, vec, vec],
        out_specs=[row, row],
        out_shape=[jax.ShapeDtypeStruct((m, d), F32), jax.ShapeDtypeStruct((m, d), BF16)],
        compiler_params=_cparams("arbitrary"),
    )(x, y, g.reshape(1, d), b.reshape(1, d))


def _sink_softmax_pv(pieces, sink):
    m = sink
    for s, _ in pieces:
        m = jnp.maximum(jnp.max(s, axis=-1, keepdims=True), m)
    ps = [jnp.exp(s - m) for s, _ in pieces]
    den = jnp.exp(sink - m)
    for p in ps:
        den = den + jnp.sum(p, axis=-1, keepdims=True)
    out = None
    for p, (_, v) in zip(ps, pieces):
        o = jnp.dot((p / den).astype(BF16), v, preferred_element_type=F32)
        out = o if out is None else out + o
    return out


def _qk(q, k):
    return lax.dot_general(q, k, (((1,), (1,)), ((), ())), preferred_element_type=F32)


def _swa_prompt_kernel(sink_ref, q_ref, kp_ref, kc_ref, vp_ref, vc_ref, o_ref, *, n_kv, group, hd, window):
    i = pl.program_id(1)
    blk = q_ref.shape[0]
    kcat = jnp.concatenate([kp_ref[...], kc_ref[...]], axis=0)
    vcat = jnp.concatenate([vp_ref[...], vc_ref[...]], axis=0)
    row = lax.broadcasted_iota(jnp.int32, (blk, 2 * blk), 0)
    col = lax.broadcasted_iota(jnp.int32, (blk, 2 * blk), 1)
    diff = row + blk - col
    mask = (diff >= 0) & (diff <= window) & ((col >= blk) | (i > 0))
    scale = hd ** -0.5
    for g in range(n_kv):
        kg = kcat[:, g * hd:(g + 1) * hd]
        vg = vcat[:, g * hd:(g + 1) * hd]
        for j in range(group):
            h = g * group + j
            s = _qk(q_ref[:, h * hd:(h + 1) * hd], kg) * scale
            s = jnp.where(mask, s, NEG_INF)
            o = _sink_softmax_pv([(s, vg)], sink_ref[h])
            o_ref[:, h * hd:(h + 1) * hd] = o.astype(o_ref.dtype)


def _swa_prompt(q, k, v, sinks, *, nseq, seq, n_kv, hd, window):
    blk = window
    nb = seq // blk
    d_q, d_kv = q.shape[1], k.shape[1]
    group = d_q // hd // n_kv
    cur = lambda b, i: (b * nb + i, 0)
    prev = lambda b, i: (b * nb + jnp.maximum(i - 1, 0), 0)
    return pl.pallas_call(
        functools.partial(_swa_prompt_kernel, n_kv=n_kv, group=group, hd=hd, window=window),
        grid=(nseq, nb),
        in_specs=[pl.BlockSpec(memory_space=pltpu.SMEM),
                  pl.BlockSpec((blk, d_q), cur),
                  pl.BlockSpec((blk, d_kv), prev), pl.BlockSpec((blk, d_kv), cur),
                  pl.BlockSpec((blk, d_kv), prev), pl.BlockSpec((blk, d_kv), cur)],
        out_specs=pl.BlockSpec((blk, d_q), cur),
        out_shape=jax.ShapeDtypeStruct(q.shape, BF16),
        compiler_params=_cparams("arbitrary", "arbitrary"),
    )(sinks, q, k, k, v, v)


def _swa_sample_kernel(sink_ref, q_ref, kn_ref, vn_ref, kc_ref, vc_ref, o_ref, *, n_kv, group, hd):
    ds = q_ref.shape[1]
    w = kc_ref.shape[1]
    kc = kc_ref[0].astype(BF16)
    vc = vc_ref[0].astype(BF16)
    kn = kn_ref[0]
    vn = vn_ref[0]
    mask_c = lax.broadcasted_iota(jnp.int32, (ds, w), 1) >= lax.broadcasted_iota(jnp.int32, (ds, w), 0)
    mask_n = lax.broadcasted_iota(jnp.int32, (ds, ds), 1) <= lax.broadcasted_iota(jnp.int32, (ds, ds), 0)
    scale = hd ** -0.5
    for g in range(n_kv):
        sl = slice(g * hd, (g + 1) * hd)
        for j in range(group):
            h = g * group + j
            qh = q_ref[0, :, h * hd:(h + 1) * hd]
            s_c = jnp.where(mask_c, _qk(qh, kc[:, sl]) * scale, NEG_INF)
            s_n = jnp.where(mask_n, _qk(qh, kn[:, sl]) * scale, NEG_INF)
            o = _sink_softmax_pv([(s_c, vc[:, sl]), (s_n, vn[:, sl])], sink_ref[h])
            o_ref[0, :, h * hd:(h + 1) * hd] = o.astype(o_ref.dtype)


def _swa_sample(q, kn, vn, kc, vc, sinks, *, n_kv, hd):
    db, ds, d_q = q.shape
    d_kv = kn.shape[2]
    w = kc.shape[1]
    group = d_q // hd // n_kv
    b3 = lambda shape: pl.BlockSpec((1,) + shape, lambda b: (b, 0, 0))
    return pl.pallas_call(
        functools.partial(_swa_sample_kernel, n_kv=n_kv, group=group, hd=hd),
        grid=(db,),
        in_specs=[pl.BlockSpec(memory_space=pltpu.SMEM), b3((ds, d_q)), b3((ds, d_kv)), b3((ds, d_kv)),
                  b3((w, d_kv)), b3((w, d_kv))],
        out_specs=b3((ds, d_q)),
        out_shape=jax.ShapeDtypeStruct(q.shape, BF16),
        compiler_params=_cparams("arbitrary"),
    )(sinks, q, kn, vn, kc, vc)


def _bf16_round(x):
    return x.astype(BF16).astype(F32)


def _conv_tail(y, cb_ref, g_ref, b_ref):
    z = _ln_rows(y + cb_ref[...], g_ref[...], b_ref[...])
    return z * jax.nn.sigmoid(z)


def _conv_prompt_kernel(up_ref, uc_ref, w_ref, cb_ref, g_ref, b_ref, o_ref, *, conv_w, halo):
    i = pl.program_id(1)
    tq = uc_ref.shape[0]
    prev = jnp.where(i > 0, up_ref[tq - halo:, :], 0.0)
    up = _bf16_round(jnp.concatenate([prev, uc_ref[...]], axis=0))
    w = w_ref[...]
    off = halo - (conv_w - 1)
    y = None
    for j in range(conv_w):
        term = up[off + j:off + j + tq, :] * w[j:j + 1, :]
        y = term if y is None else y + term
    o_ref[...] = _conv_tail(y, cb_ref, g_ref, b_ref).astype(o_ref.dtype)


def _conv_prompt(u, conv_w, conv_b, ln_g, ln_b, *, nseq, seq):
    c = u.shape[1]
    cw = conv_w.shape[0]
    halo = -(-(cw - 1) // 8) * 8
    tq = _tile(seq, 256, BF16_SUBLANES)
    nb = seq // tq
    cur = lambda b, i: (b * nb + i, 0)
    prev = lambda b, i: (b * nb + jnp.maximum(i - 1, 0), 0)
    vec = pl.BlockSpec((1, c), lambda b, i: (0, 0))
    return pl.pallas_call(
        functools.partial(_conv_prompt_kernel, conv_w=cw, halo=halo),
        grid=(nseq, nb),
        in_specs=[pl.BlockSpec((tq, c), prev), pl.BlockSpec((tq, c), cur),
                  pl.BlockSpec((cw, c), lambda b, i: (0, 0)), vec, vec, vec],
        out_specs=pl.BlockSpec((tq, c), cur),
        out_shape=jax.ShapeDtypeStruct(u.shape, BF16),
        compiler_params=_cparams("arbitrary", "arbitrary"),
    )(u, u, conv_w, conv_b.reshape(1, c), ln_g.reshape(1, c), ln_b.reshape(1, c))


def _conv_sample_kernel(up_ref, w_ref, cb_ref, g_ref, b_ref, o_ref, *, conv_w):
    ds = o_ref.shape[1]
    up = _bf16_round(up_ref[0])
    w = w_ref[...]
    y = None
    for j in range(conv_w):
        term = up[j:j + ds, :] * w[j:j + 1, :]
        y = term if y is None else y + term
    o_ref[0] = _conv_tail(y, cb_ref, g_ref, b_ref).astype(o_ref.dtype)


def _conv_sample(up, conv_w, conv_b, ln_g, ln_b, *, ds):
    db, length, c = up.shape
    cw = conv_w.shape[0]
    vec = pl.BlockSpec((1, c), lambda b: (0, 0))
    return pl.pallas_call(
        functools.partial(_conv_sample_kernel, conv_w=cw),
        grid=(db,),
        in_specs=[pl.BlockSpec((1, length, c), lambda b: (b, 0, 0)), pl.BlockSpec((cw, c), lambda b: (0, 0)),
                  vec, vec, vec],
        out_specs=pl.BlockSpec((1, ds, c), lambda b: (b, 0, 0)),
        out_shape=jax.ShapeDtypeStruct((db, ds, c), BF16),
        compiler_params=_cparams("arbitrary"),
    )(up, conv_w, conv_b.reshape(1, c), ln_g.reshape(1, c), ln_b.reshape(1, c))


def _wkv_kernel(r_ref, d_ref, k_ref, v_ref, q_ref, p_ref, s0_ref, o_ref, st_ref, s_scr, *, heads, hc):
    c = pl.program_id(2)
    steps = r_ref.shape[1]

    @pl.when(c == 0)
    def _():
        s_scr[...] = s0_ref[0]

    eye = lax.broadcasted_iota(jnp.int32, (hc, hc), 0) == lax.broadcasted_iota(jnp.int32, (hc, hc), 1)

    rows = min(steps, 8)

    def group(gi, carry):
        base = pl.multiple_of(gi * rows, rows)
        for h in range(heads):
            sl = slice(h * hc, (h + 1) * hc)
            r, d, k, v, q, p = [ref[0, pl.ds(base, rows), sl] for ref in (r_ref, d_ref, k_ref, v_ref, q_ref, p_ref)]
            r, q = _bf16_round(r), _bf16_round(q)
            s = s_scr[h]
            sb = _bf16_round(s)
            outs = []
            for j in range(rows):
                row = lambda z: z[j:j + 1, :]
                v_col = jnp.sum(jnp.where(eye, row(v), 0.0), axis=1, keepdims=True)
                u = jnp.sum(sb * row(q), axis=1, keepdims=True)
                s = s * row(d) - u * row(p) + v_col * row(k)
                sb = _bf16_round(s)
                o_col = jnp.sum(sb * row(r), axis=1, keepdims=True)
                outs.append(jnp.sum(jnp.where(eye, o_col, 0.0), axis=0, keepdims=True))
            o_ref[0, pl.ds(base, rows), sl] = jnp.concatenate(outs, axis=0)
            s_scr[h] = s
        return carry

    lax.fori_loop(0, steps // rows, group, 0)

    @pl.when(c == pl.num_programs(2) - 1)
    def _():
        st_ref[0] = s_scr[...]


def _wkv(r, d, k, v, q, p, s0, *, hc):
    nseq, t, dm = r.shape
    heads = LANES // hc * 4
    heads = _tile(dm // hc, heads, 1)
    wd = heads * hc
    tc = _tile(t, 256, 8)
    seq_spec = pl.BlockSpec((1, tc, wd), lambda b, g, c: (b, c, g))
    st_spec = pl.BlockSpec((1, heads, hc, hc), lambda b, g, c: (b, g, 0, 0))
    return pl.pallas_call(
        functools.partial(_wkv_kernel, heads=heads, hc=hc),
        grid=(nseq, dm // wd, t // tc),
        in_specs=[seq_spec] * 6 + [st_spec],
        out_specs=[seq_spec, st_spec],
        out_shape=[jax.ShapeDtypeStruct(r.shape, F32), jax.ShapeDtypeStruct(s0.shape, F32)],
        scratch_shapes=[pltpu.VMEM((heads, hc, hc), F32)],
        compiler_params=_cparams("arbitrary", "arbitrary", "arbitrary"),
    )(r, d, k, v, q, p, s0)


def _router_kernel(x_ref, w_ref, idx_ref, gate_ref, *, n_exp):
    logits = jnp.dot(x_ref[...].astype(BF16), w_ref[...].astype(BF16), preferred_element_type=F32)
    lane_i = lax.broadcasted_iota(jnp.int32, logits.shape, 1)
    lane = lane_i.astype(F32)
    logits = jnp.where(lane_i < n_exp, logits, NEG_INF)
    m1 = jnp.max(logits, axis=-1, keepdims=True)
    i1 = jnp.min(jnp.where(logits == m1, lane, float(LANES)), axis=-1, keepdims=True)
    rest = jnp.where(lane == i1, NEG_INF, logits)
    m2 = jnp.max(rest, axis=-1, keepdims=True)
    i2 = jnp.min(jnp.where(rest == m2, lane, float(LANES)), axis=-1, keepdims=True)
    e2 = jnp.exp(m2 - m1)
    den = 1.0 + e2
    idx_ref[...] = jnp.where(lane_i == 0, i1, i2).astype(jnp.int32)
    gate_ref[...] = jnp.where(lane == 0, 1.0 / den, e2 / den)


def _router(x, w_router):
    m, d = x.shape
    n_exp = w_router.shape[1]
    w = jnp.pad(w_router, ((0, 0), (0, LANES - n_exp)))
    tm = _tile(m, 512, 8)
    out = pl.BlockSpec((tm, LANES), lambda i: (i, 0))
    idx, gate = pl.pallas_call(
        functools.partial(_router_kernel, n_exp=n_exp),
        grid=(m // tm,),
        in_specs=[pl.BlockSpec((tm, d), lambda i: (i, 0)), pl.BlockSpec((d, LANES), lambda i: (0, 0))],
        out_specs=[out, out],
        out_shape=[jax.ShapeDtypeStruct((m, LANES), jnp.int32), jax.ShapeDtypeStruct((m, LANES), F32)],
        compiler_params=_cparams("arbitrary"),
    )(x, w)
    return idx[:, :TOP_K], gate[:, :TOP_K]


def _gather_kernel(src_ref, x_hbm, o_ref, buf, sem):
    tr = o_ref.shape[0]
    base = pl.program_id(0) * tr

    def row_copy(i):
        return pltpu.make_async_copy(x_hbm.at[pl.ds(src_ref[base + i], 1)], buf.at[pl.ds(i, 1)], sem)

    def start(i, c):
        row_copy(i).start()
        return c

    def wait(i, c):
        row_copy(i).wait()
        return c

    lax.fori_loop(0, tr, start, 0)
    lax.fori_loop(0, tr, wait, 0)
    o_ref[...] = buf[...].astype(o_ref.dtype)


def _gather_rows(x, src, *, tr):
    rows = src.shape[0]
    d = x.shape[1]
    return pl.pallas_call(
        _gather_kernel,
        grid_spec=pltpu.PrefetchScalarGridSpec(
            num_scalar_prefetch=1,
            grid=(rows // tr,),
            in_specs=[pl.BlockSpec(memory_space=pl.ANY)],
            out_specs=pl.BlockSpec((tr, d), lambda i, src: (i, 0)),
            scratch_shapes=[pltpu.VMEM((tr, d), F32), pltpu.SemaphoreType.DMA(())],
        ),
        out_shape=jax.ShapeDtypeStruct((rows, d), BF16),
        compiler_params=_cparams("arbitrary"),
    )(src, x)


def _combine_ln_kernel(pos_ref, y_hbm, x_ref, gate_ref, g_ref, b_ref, o_ref, buf, sem, *, alpha):
    tr = o_ref.shape[0]
    base = pl.program_id(0) * tr

    def row_copy(i, slot):
        src = pos_ref[(base + i) * TOP_K + slot]
        return pltpu.make_async_copy(y_hbm.at[pl.ds(src, 1)], buf.at[slot, pl.ds(i, 1)], sem)

    def start(i, c):
        for slot in range(TOP_K):
            row_copy(i, slot).start()
        return c

    def wait(i, c):
        for slot in range(TOP_K):
            row_copy(i, slot).wait()
        return c

    lax.fori_loop(0, tr, start, 0)
    lax.fori_loop(0, tr, wait, 0)
    gates = gate_ref[...]
    f = gates[:, 0:1] * buf[0] + gates[:, 1:2] * buf[1]
    o_ref[...] = _ln_rows(alpha * x_ref[...] + f, g_ref[...], b_ref[...])


def _combine_ln(y_sorted, pos, gates, x, g, b, alpha):
    m, d = x.shape
    tr = _tile(m, 128, 8)
    row = pl.BlockSpec((tr, d), lambda i, pos: (i, 0))
    vec = pl.BlockSpec((1, d), lambda i, pos: (0, 0))
    return pl.pallas_call(
        functools.partial(_combine_ln_kernel, alpha=alpha),
        grid_spec=pltpu.PrefetchScalarGridSpec(
            num_scalar_prefetch=1,
            grid=(m // tr,),
            in_specs=[pl.BlockSpec(memory_space=pl.ANY), row,
                      pl.BlockSpec((tr, TOP_K), lambda i, pos: (i, 0)), vec, vec],
            out_specs=row,
            scratch_shapes=[pltpu.VMEM((TOP_K, tr, d), F32), pltpu.SemaphoreType.DMA(())],
        ),
        out_shape=jax.ShapeDtypeStruct((m, d), F32),
        compiler_params=_cparams("arbitrary"),
    )(pos, y_sorted, x, gates, g.reshape(1, d), b.reshape(1, d))


def _route_tables(top_i, n_exp, tm):
    m = top_i.shape[0]
    n_rows = m * TOP_K
    n_tiles = -(-(n_rows + n_exp * (tm - 1)) // tm)
    e_flat = top_i.reshape(-1)
    order = jnp.argsort(e_flat, stable=True)
    counts = jnp.sum(e_flat[:, None] == jnp.arange(n_exp, dtype=jnp.int32)[None, :], axis=0).astype(jnp.int32)
    padded = (counts + tm - 1) // tm * tm
    pad_end = jnp.cumsum(padded)
    pad_start = pad_end - padded
    cnt_start = jnp.cumsum(counts) - counts
    e_sorted = e_flat[order]
    dest_sorted = pad_start[e_sorted] + jnp.arange(n_rows, dtype=jnp.int32) - cnt_start[e_sorted]
    pos = jnp.zeros((n_rows,), jnp.int32).at[order].set(dest_sorted)
    src = jnp.zeros((n_tiles * tm,), jnp.int32).at[dest_sorted].set((order // TOP_K).astype(jnp.int32))
    n_valid = pad_end[-1] // tm
    t_idx = jnp.arange(n_tiles, dtype=jnp.int32)
    t_eff = jnp.minimum(t_idx, jnp.maximum(n_valid - 1, 0))
    te = jnp.searchsorted(pad_end, t_eff * tm, side="right").astype(jnp.int32)
    te = jnp.minimum(te, n_exp - 1)
    tf = jnp.concatenate([jnp.ones((1,), jnp.int32), (te[1:] != te[:-1]).astype(jnp.int32)])
    tv = (t_idx < n_valid).astype(jnp.int32)
    return src, pos, (te, tf, tv)


def _rope(x, pos, hd):
    half = hd // 2
    inv = jnp.power(ROPE_THETA, -jnp.arange(half, dtype=F32) / half)
    ang = pos.astype(F32)[:, None] * inv[None, :]
    cos, sin = jnp.cos(ang)[:, None, :], jnp.sin(ang)[:, None, :]
    xh = x.reshape(x.shape[0], -1, hd)
    x1, x2 = xh[..., :half], xh[..., half:]
    return jnp.concatenate([x1 * cos - x2 * sin, x2 * cos + x1 * sin], axis=-1).reshape(x.shape)


def _even_mixer(x_bf, dims, cache_k, cache_v, state_conv, w_in, b_in, sinks, conv_w, conv_b, conv_g, conv_bn,
                w_out, b_out):
    nseq, seq, db, ds, n_p = dims
    m = x_bf.shape[0]
    _, w, n_kv, hd = cache_k.shape
    d_kv = n_kv * hd
    d_q = sinks.shape[0] * hd
    d_conv = state_conv.shape[-1]
    cw = conv_w.shape[0]
    tm = _tile(m, 640, BF16_SUBLANES)
    h = _dense(x_bf, w_in, tm=tm, tn=_tile(w_in.shape[1], 512, LANES), bias=b_in)
    pos = jnp.concatenate([jnp.tile(jnp.arange(seq, dtype=jnp.int32), nseq),
                           jnp.tile(PAST_LEN + jnp.arange(ds, dtype=jnp.int32), db)])
    q = _rope(h[:, :d_q], pos, hd).astype(BF16)
    k = _rope(h[:, d_q:d_q + d_kv], pos, hd)
    v = h[:, d_q + d_kv:d_q + 2 * d_kv]
    a, gate = h[:, d_q + 2 * d_kv:d_q + 2 * d_kv + d_conv], h[:, d_q + 2 * d_kv + d_conv:]
    u = a * jax.nn.sigmoid(gate)
    k_bf, v_bf = k.astype(BF16), v.astype(BF16)

    att_p = _swa_prompt(q[:n_p], k_bf[:n_p], v_bf[:n_p], sinks, nseq=nseq, seq=seq, n_kv=n_kv, hd=hd, window=w)
    cache_k2 = cache_k.reshape(db, w, d_kv)
    cache_v2 = cache_v.reshape(db, w, d_kv)
    att_s = _swa_sample(q[n_p:].reshape(db, ds, d_q), k_bf[n_p:].reshape(db, ds, d_kv),
                        v_bf[n_p:].reshape(db, ds, d_kv), cache_k2, cache_v2, sinks, n_kv=n_kv, hd=hd)
    conv_p = _conv_prompt(u[:n_p], conv_w, conv_b, conv_g, conv_bn, nseq=nseq, seq=seq)
    up_s = jnp.concatenate([state_conv, u[n_p:].reshape(db, ds, d_conv)], axis=1)
    conv_s = _conv_sample(up_s, conv_w, conv_b, conv_g, conv_bn, ds=ds)

    mix = jnp.concatenate([jnp.concatenate([att_p, att_s.reshape(db * ds, d_q)], axis=0),
                           jnp.concatenate([conv_p, conv_s.reshape(db * ds, d_conv)], axis=0)], axis=1)
    y = _dense(mix, w_out, tm=tm, tn=_tile(w_out.shape[1], 512, LANES), bias=b_out)

    kp = k[:n_p].reshape(nseq, seq, n_kv, hd)[:, -w:]
    vp = v[:n_p].reshape(nseq, seq, n_kv, hd)[:, -w:]
    ks = jnp.concatenate([cache_k, k[n_p:].reshape(db, ds, n_kv, hd)], axis=1)[:, -w:]
    vs = jnp.concatenate([cache_v, v[n_p:].reshape(db, ds, n_kv, hd)], axis=1)[:, -w:]
    cp = u[:n_p].reshape(nseq, seq, d_conv)[:, -(cw - 1):]
    cs = up_s[:, -(cw - 1):]
    return y, (kp, vp, ks, vs, cp, cs)


def _rwkv_mixer(x, dims, state_shift, state_wkv, mu, w0, w1, w2, a0, a1, a2, g1, g2, k_k, k_a, r_k, w_r, w_k, w_v,
                w_o, lnx_g, lnx_b):
    nseq, seq, db, ds, n_p = dims
    m, d = x.shape
    n_heads, hc = state_wkv.shape[1], state_wkv.shape[-1]
    xp = x[:n_p].reshape(nseq, seq, d)
    xs = x[n_p:].reshape(db, ds, d)
    prev_p = jnp.concatenate([jnp.zeros((nseq, 1, d), F32), xp[:, :-1]], axis=1)
    prev_s = jnp.concatenate([state_shift[:, None], xs[:, :-1]], axis=1)
    prev = jnp.concatenate([prev_p.reshape(n_p, d), prev_s.reshape(db * ds, d)], axis=0)
    xx = prev - x
    xr, xw, xk, xv, xa, xg = [(x + xx * mu[i]).astype(BF16) for i in range(6)]
    tm = _tile(m, 640, BF16_SUBLANES)
    big = lambda z, wt: _dense(z, wt, tm=tm, tn=_tile(wt.shape[1], 512, LANES))
    r = big(xr, w_r)
    k = big(xk, w_k)
    v = big(xv, w_v)
    lw = big(jnp.tanh(big(xw, w1)).astype(BF16), w2)
    la = big(big(xa, a1).astype(BF16), a2)
    g = big(jax.nn.sigmoid(big(xg, g1)).astype(BF16), g2)
    w = -jax.nn.softplus(-(w0 + lw)) - 0.5
    a = jax.nn.sigmoid(a0 + la)
    heads = lambda z: z.reshape(m, n_heads, hc)
    kk = heads(k * k_k)
    kk = kk * lax.rsqrt(jnp.maximum(jnp.sum(kk * kk, axis=-1, keepdims=True), 1e-24))
    k = k * (1 + (a - 1) * k_a)
    decay = jnp.exp(-jnp.exp(w))
    kk = kk.reshape(m, d)
    p = kk * a

    seqs = lambda z, sl, shape: z[sl].reshape(shape)
    wkv_in = (r, decay, k, v, kk, p)
    args_p = [seqs(z, slice(0, n_p), (nseq, seq, d)) for z in wkv_in]
    args_s = [seqs(z, slice(n_p, m), (db, ds, d)) for z in wkv_in]
    o_p, wkv_p = _wkv(*args_p, jnp.zeros((nseq, n_heads, hc, hc), F32), hc=hc)
    o_s, wkv_s = _wkv(*args_s, state_wkv, hc=hc)
    o = heads(jnp.concatenate([o_p.reshape(n_p, d), o_s.reshape(db * ds, d)], axis=0))

    o_mean = jnp.mean(o, axis=-1, keepdims=True)
    o_var = jnp.mean(jnp.square(o - o_mean), axis=-1, keepdims=True)
    o = ((o - o_mean) * lax.rsqrt(o_var + GN_EPS)).reshape(m, d) * lnx_g + lnx_b
    bonus = jnp.sum(heads(r) * heads(k) * r_k, axis=-1, keepdims=True) * heads(v)
    o = o + bonus.reshape(m, d)
    y = big((o * g).astype(BF16), w_o)
    return y, (xp[:, -1], xs[:, -1], wkv_p, wkv_s)


def _swiglu_dense(x_bf, wg, wu, wd, *, tm):
    d_ff = wg.shape[1]
    h = _mm(x_bf, [wg[None], wu[None]], tm=tm, tn=_tile(d_ff, 256, LANES), out_dtype=BF16)
    nk = 2 if d_ff % (2 * LANES) == 0 and d_ff > 8192 else 1
    tm_d = _tile(x_bf.shape[0], 320, BF16_SUBLANES) if nk > 1 else tm
    return _dense(h, wd, tm=tm_d, tn=_tile(wd.shape[1], 512, LANES), nk=nk)


def _moe(x, x_ln_args, w_router, wg, wu, wd, alpha):
    m, d = x.shape
    n_exp, _, d_ff = wg.shape
    tm = 512 if m * TOP_K >= 4096 else 64
    top_i, gates = _router(x, w_router)
    src, pos, tiles = _route_tables(top_i, n_exp, tm)
    xs = _gather_rows(x, src, tr=_tile(src.shape[0], 256, BF16_SUBLANES))
    h = _mm(xs, [wg, wu], tm=tm, tn=_tile(d_ff, 256, LANES), out_dtype=BF16, tiles=tiles)
    nk = 4 if d_ff % (4 * LANES) == 0 and d_ff > 8192 else 1
    y = None
    for kk in range(nk):
        y = _mm(h, [wd], tm=tm, tn=_tile(d, 512, LANES), nk=nk, kk=kk, add=y, tiles=tiles)
    g, b = x_ln_args
    return _combine_ln(y, pos, gates, x, g, b, alpha)


def kernel(x_prompt, x_sample, cache_swa_k, cache_swa_v, state_conv, state_shift, state_wkv, ln_g, ln_b, w_in, b_in,
           sinks, conv_w, conv_b, conv_ln_g, conv_ln_b, w_out, b_out, ffn_w_gate, ffn_w_up, ffn_w_down, mu, w0, w1,
           w2, a0, a1, a2, g1, g2, k_k, k_a, r_k, w_r, w_k, w_v, w_o, lnx_g, lnx_b, w_router, moe_w_gate, moe_w_up,
           moe_w_down):
    nseq, seq, d = x_prompt.shape
    db, ds, _ = x_sample.shape
    n_p = nseq * seq
    dims = (nseq, seq, db, ds, n_p)
    depth = ln_g.shape[0]
    alpha = (2 * depth) ** 0.25
    x = jnp.concatenate([x_prompt.reshape(n_p, d), x_sample.reshape(db * ds, d)], axis=0)
    m = x.shape[0]
    tm = _tile(m, 640, BF16_SUBLANES)
    x_bf = x.astype(BF16)
    even_out, odd_out = [], []
    for layer in range(depth):
        i = layer // 2
        if layer % 2 == 0:
            y, states = _even_mixer(x_bf, dims, cache_swa_k[i], cache_swa_v[i], state_conv[i], w_in[i], b_in[i],
                                    sinks[i], conv_w[i], conv_b[i], conv_ln_g[i], conv_ln_b[i], w_out[i], b_out[i])
            even_out.append(states)
        else:
            y, states = _rwkv_mixer(x, dims, state_shift[i], state_wkv[i], mu[i], w0[i], w1[i], w2[i], a0[i], a1[i],
                                    a2[i], g1[i], g2[i], k_k[i], k_a[i], r_k[i], w_r[i], w_k[i], w_v[i], w_o[i],
                                    lnx_g[i], lnx_b[i])
            odd_out.append(states)
        x, x_bf = _add_ln(x, y, ln_g[layer, 0], ln_b[layer, 0], alpha)
        if layer % 2 == 0:
            f = _swiglu_dense(x_bf, ffn_w_gate[i], ffn_w_up[i], ffn_w_down[i], tm=tm)
            x, x_bf = _add_ln(x, f, ln_g[layer, 1], ln_b[layer, 1], alpha)
        else:
            x = _moe(x, (ln_g[layer, 1], ln_b[layer, 1]), w_router[i], moe_w_gate[i], moe_w_up[i], moe_w_down[i],
                     alpha)
            x_bf = x.astype(BF16)
    stack = lambda outs, j: jnp.stack([o[j] for o in outs])
    return (x[:n_p].reshape(nseq, seq, d), x[n_p:].reshape(db, ds, d),
            stack(even_out, 0), stack(even_out, 1), stack(even_out, 2), stack(even_out, 3),
            stack(even_out, 4), stack(even_out, 5),
            stack(odd_out, 0), stack(odd_out, 1), stack(odd_out, 2), stack(odd_out, 3))
```

```python
import functools

import jax
import jax.numpy as jnp
from jax import lax
from jax.experimental import pallas as pl
from jax.experimental.pallas import tpu as pltpu

F32 = jnp.float32
BF16 = jnp.bfloat16

PAST_LEN = 16384
ROPE_THETA = 10000.0
LN_EPS = 1e-5
GN_EPS = 64e-5
TOP_K = 2
NEG_INF = float("-inf")

VMEM_LIMIT_BYTES = 58 * 1024 * 1024
LANES = 128
BF16_SUBLANES = 16


def _cparams(*sem):
    return pltpu.CompilerParams(dimension_semantics=sem, vmem_limit_bytes=VMEM_LIMIT_BYTES)


def _tile(n, target, mult):
    best = None
    for t in range(mult, min(n, target) + 1, mult):
        if n % t == 0:
            best = t
    return best if best is not None else n


def _mm_kernel(te_ref, tf_ref, tv_ref, *refs, n_w, has_bias, has_add):
    del te_ref
    x_ref = refs[0]
    w_refs = refs[1:1 + n_w]
    pos = 1 + n_w
    b_ref = a_ref = None
    if has_bias:
        b_ref = refs[pos]
        pos += 1
    if has_add:
        a_ref = refs[pos]
        pos += 1
    o_ref = refs[pos]
    wb_refs = refs[pos + 1:pos + 1 + n_w]
    t = pl.program_id(1)

    @pl.when(tf_ref[t] == 1)
    def _():
        for w_ref, wb_ref in zip(w_refs, wb_refs):
            wb_ref[...] = w_ref[0].astype(BF16)

    @pl.when(tv_ref[t] == 1)
    def _():
        x = x_ref[...]
        acc = jnp.dot(x, wb_refs[0][...], preferred_element_type=F32)
        if n_w == 2:
            up = jnp.dot(x, wb_refs[1][...], preferred_element_type=F32)
            acc = acc * jax.nn.sigmoid(acc) * up
        if has_bias:
            acc = acc + b_ref[...]
        if has_add:
            acc = acc + a_ref[...]
        o_ref[...] = acc.astype(o_ref.dtype)

    @pl.when(tv_ref[t] == 0)
    def _():
        o_ref[...] = jnp.zeros(o_ref.shape, o_ref.dtype)


def _mm(x, ws, *, tm, tn, nk=1, kk=0, bias=None, add=None, out_dtype=F32, tiles=None):
    m, k = x.shape
    n = ws[0].shape[-1]
    assert m % tm == 0 and n % tn == 0 and k % nk == 0
    tk = k // nk
    nt = m // tm
    if tiles is None:
        te = jnp.zeros((nt,), jnp.int32)
        tf = jnp.zeros((nt,), jnp.int32).at[0].set(1)
        tv = jnp.ones((nt,), jnp.int32)
    else:
        te, tf, tv = tiles
    n_w = len(ws)
    in_specs = [pl.BlockSpec((tm, tk), lambda j, t, te, tf, tv: (t, kk))]
    for _ in ws:
        in_specs.append(pl.BlockSpec((1, tk, tn), lambda j, t, te, tf, tv: (te[t], kk, j)))
    args = [x, *ws]
    if bias is not None:
        in_specs.append(pl.BlockSpec((1, tn), lambda j, t, te, tf, tv: (0, j)))
        args.append(bias.reshape(1, n))
    if add is not None:
        in_specs.append(pl.BlockSpec((tm, tn), lambda j, t, te, tf, tv: (t, j)))
        args.append(add)
    kern = functools.partial(_mm_kernel, n_w=n_w, has_bias=bias is not None, has_add=add is not None)
    return pl.pallas_call(
        kern,
        grid_spec=pltpu.PrefetchScalarGridSpec(
            num_scalar_prefetch=3,
            grid=(n // tn, nt),
            in_specs=in_specs,
            out_specs=pl.BlockSpec((tm, tn), lambda j, t, te, tf, tv: (t, j)),
            scratch_shapes=[pltpu.VMEM((tk, tn), BF16) for _ in ws],
        ),
        out_shape=jax.ShapeDtypeStruct((m, n), out_dtype),
        compiler_params=_cparams("arbitrary", "arbitrary"),
    )(te, tf, tv, *args)


def _dense(x, w, *, tm, tn, nk=1, bias=None, out_dtype=F32):
    out = None
    for kk in range(nk):
        last = kk == nk - 1
        out = _mm(x, [w[None]], tm=tm, tn=tn, nk=nk, kk=kk, bias=bias if kk == 0 else None, add=out,
                  out_dtype=out_dtype if last else F32)
    return out


def _ln_rows(z, g, b):
    mean = jnp.mean(z, axis=-1, keepdims=True)
    zc = z - mean
    var = jnp.mean(zc * zc, axis=-1, keepdims=True)
    return zc * lax.rsqrt(var + LN_EPS) * g + b


def _add_ln_kernel(x_ref, y_ref, g_ref, b_ref, o_ref, ob_ref, *, alpha):
    out = _ln_rows(alpha * x_ref[...] + y_ref[...], g_ref[...], b_ref[...])
    o_ref[...] = out
    ob_ref[...] = out.astype(BF16)


def _add_ln(x, y, g, b, alpha):
    m, d = x.shape
    tm = _tile(m, 256, BF16_SUBLANES)
    row = pl.BlockSpec((tm, d), lambda i: (i, 0))
    vec = pl.BlockSpec((1, d), lambda i: (0, 0))
    return pl.pallas_call(
        functools.partial(_add_ln_kernel, alpha=alpha),
        grid=(m // tm,),
        in_specs=[row, row, vec, vec],
        out_specs=[row, row],
        out_shape=[jax.ShapeDtypeStruct((m, d), F32), jax.ShapeDtypeStruct((m, d), BF16)],
        compiler_params=_cparams("arbitrary"),
    )(x, y, g.reshape(1, d), b.reshape(1, d))


def _sink_softmax_pv(pieces, sink):
    m = sink
    for s, _ in pieces:
        m = jnp.maximum(jnp.max(s, axis=-1, keepdims=True), m)
    ps = [jnp.exp(s - m) for s, _ in pieces]
    den = jnp.exp(sink - m)
    for p in ps:
        den = den + jnp.sum(p, axis=-1, keepdims=True)
    out = None
    for p, (_, v) in zip(ps, pieces):
        o = jnp.dot((p / den).astype(BF16), v, preferred_element_type=F32)
        out = o if out is None else out + o
    return out


def _qk(q, k):
    return lax.dot_general(q, k, (((1,), (1,)), ((), ())), preferred_element_type=F32)


def _swa_prompt_kernel(sink_ref, q_ref, kp_ref, kc_ref, vp_ref, vc_ref, o_ref, *, n_kv, group, hd, window):
    i = pl.program_id(1)
    blk = q_ref.shape[0]
    kcat = jnp.concatenate([kp_ref[...], kc_ref[...]], axis=0)
    vcat = jnp.concatenate([vp_ref[...], vc_ref[...]], axis=0)
    row = lax.broadcasted_iota(jnp.int32, (blk, 2 * blk), 0)
    col = lax.broadcasted_iota(jnp.int32, (blk, 2 * blk), 1)
    diff = row + blk - col
    mask = (diff >= 0) & (diff <= window) & ((col >= blk) | (i > 0))
    scale = hd ** -0.5
    for g in range(n_kv):
        kg = kcat[:, g * hd:(g + 1) * hd]
        vg = vcat[:, g * hd:(g + 1) * hd]
        for j in range(group):
            h = g * group + j
            s = _qk(q_ref[:, h * hd:(h + 1) * hd], kg) * scale
            s = jnp.where(mask, s, NEG_INF)
            o = _sink_softmax_pv([(s, vg)], sink_ref[h])
            o_ref[:, h * hd:(h + 1) * hd] = o.astype(o_ref.dtype)


def _swa_prompt(q, k, v, sinks, *, nseq, seq, n_kv, hd, window):
    blk = window
    nb = seq // blk
    d_q, d_kv = q.shape[1], k.shape[1]
    group = d_q // hd // n_kv
    cur = lambda b, i: (b * nb + i, 0)
    prev = lambda b, i: (b * nb + jnp.maximum(i - 1, 0), 0)
    return pl.pallas_call(
        functools.partial(_swa_prompt_kernel, n_kv=n_kv, group=group, hd=hd, window=window),
        grid=(nseq, nb),
        in_specs=[pl.BlockSpec(memory_space=pltpu.SMEM),
                  pl.BlockSpec((blk, d_q), cur),
                  pl.BlockSpec((blk, d_kv), prev), pl.BlockSpec((blk, d_kv), cur),
                  pl.BlockSpec((blk, d_kv), prev), pl.BlockSpec((blk, d_kv), cur)],
        out_specs=pl.BlockSpec((blk, d_q), cur),
        out_shape=jax.ShapeDtypeStruct(q.shape, BF16),
        compiler_params=_cparams("arbitrary", "arbitrary"),
    )(sinks, q, k, k, v, v)


def _swa_sample_kernel(sink_ref, q_ref, kn_ref, vn_ref, kc_ref, vc_ref, o_ref, *, n_kv, group, hd):
    ds = q_ref.shape[1]
    w = kc_ref.shape[1]
    kc = kc_ref[0].astype(BF16)
    vc = vc_ref[0].astype(BF16)
    kn = kn_ref[0]
    vn = vn_ref[0]
    mask_c = lax.broadcasted_iota(jnp.int32, (ds, w), 1) >= lax.broadcasted_iota(jnp.int32, (ds, w), 0)
    mask_n = lax.broadcasted_iota(jnp.int32, (ds, ds), 1) <= lax.broadcasted_iota(jnp.int32, (ds, ds), 0)
    scale = hd ** -0.5
    for g in range(n_kv):
        sl = slice(g * hd, (g + 1) * hd)
        for j in range(group):
            h = g * group + j
            qh = q_ref[0, :, h * hd:(h + 1) * hd]
            s_c = jnp.where(mask_c, _qk(qh, kc[:, sl]) * scale, NEG_INF)
            s_n = jnp.where(mask_n, _qk(qh, kn[:, sl]) * scale, NEG_INF)
            o = _sink_softmax_pv([(s_c, vc[:, sl]), (s_n, vn[:, sl])], sink_ref[h])
            o_ref[0, :, h * hd:(h + 1) * hd] = o.astype(o_ref.dtype)


def _swa_sample(q, kn, vn, kc, vc, sinks, *, n_kv, hd):
    db, ds, d_q = q.shape
    d_kv = kn.shape[2]
    w = kc.shape[1]
    group = d_q // hd // n_kv
    b3 = lambda shape: pl.BlockSpec((1,) + shape, lambda b: (b, 0, 0))
    return pl.pallas_call(
        functools.partial(_swa_sample_kernel, n_kv=n_kv, group=group, hd=hd),
        grid=(db,),
        in_specs=[pl.BlockSpec(memory_space=pltpu.SMEM), b3((ds, d_q)), b3((ds, d_kv)), b3((ds, d_kv)),
                  b3((w, d_kv)), b3((w, d_kv))],
        out_specs=b3((ds, d_q)),
        out_shape=jax.ShapeDtypeStruct(q.shape, BF16),
        compiler_params=_cparams("arbitrary"),
    )(sinks, q, kn, vn, kc, vc)


def _bf16_round(x):
    return x.astype(BF16).astype(F32)


def _conv_tail(y, cb_ref, g_ref, b_ref):
    z = _ln_rows(y + cb_ref[...], g_ref[...], b_ref[...])
    return z * jax.nn.sigmoid(z)


def _conv_prompt_kernel(up_ref, uc_ref, w_ref, cb_ref, g_ref, b_ref, o_ref, *, conv_w, halo):
    i = pl.program_id(1)
    tq = uc_ref.shape[0]
    prev = jnp.where(i > 0, up_ref[tq - halo:, :], 0.0)
    up = _bf16_round(jnp.concatenate([prev, uc_ref[...]], axis=0))
    w = w_ref[...]
    off = halo - (conv_w - 1)
    y = None
    for j in range(conv_w):
        term = up[off + j:off + j + tq, :] * w[j:j + 1, :]
        y = term if y is None else y + term
    o_ref[...] = _conv_tail(y, cb_ref, g_ref, b_ref).astype(o_ref.dtype)


def _conv_prompt(u, conv_w, conv_b, ln_g, ln_b, *, nseq, seq):
    c = u.shape[1]
    cw = conv_w.shape[0]
    halo = -(-(cw - 1) // 8) * 8
    tq = _tile(seq, 256, BF16_SUBLANES)
    nb = seq // tq
    cur = lambda b, i: (b * nb + i, 0)
    prev = lambda b, i: (b * nb + jnp.maximum(i - 1, 0), 0)
    vec = pl.BlockSpec((1, c), lambda b, i: (0, 0))
    return pl.pallas_call(
        functools.partial(_conv_prompt_kernel, conv_w=cw, halo=halo),
        grid=(nseq, nb),
        in_specs=[pl.BlockSpec((tq, c), prev), pl.BlockSpec((tq, c), cur),
                  pl.BlockSpec((cw, c), lambda b, i: (0, 0)), vec, vec, vec],
        out_specs=pl.BlockSpec((tq, c), cur),
        out_shape=jax.ShapeDtypeStruct(u.shape, BF16),
        compiler_params=_cparams("arbitrary", "arbitrary"),
    )(u, u, conv_w, conv_b.reshape(1, c), ln_g.reshape(1, c), ln_b.reshape(1, c))


def _conv_sample_kernel(up_ref, w_ref, cb_ref, g_ref, b_ref, o_ref, *, conv_w):
    ds = o_ref.shape[1]
    up = _bf16_round(up_ref[0])
    w = w_ref[...]
    y = None
    for j in range(conv_w):
        term = up[j:j + ds, :] * w[j:j + 1, :]
        y = term if y is None else y + term
    o_ref[0] = _conv_tail(y, cb_ref, g_ref, b_ref).astype(o_ref.dtype)


def _conv_sample(up, conv_w, conv_b, ln_g, ln_b, *, ds):
    db, length, c = up.shape
    cw = conv_w.shape[0]
    vec = pl.BlockSpec((1, c), lambda b: (0, 0))
    return pl.pallas_call(
        functools.partial(_conv_sample_kernel, conv_w=cw),
        grid=(db,),
        in_specs=[pl.BlockSpec((1, length, c), lambda b: (b, 0, 0)), pl.BlockSpec((cw, c), lambda b: (0, 0)),
                  vec, vec, vec],
        out_specs=pl.BlockSpec((1, ds, c), lambda b: (b, 0, 0)),
        out_shape=jax.ShapeDtypeStruct((db, ds, c), BF16),
        compiler_params=_cparams("arbitrary"),
    )(up, conv_w, conv_b.reshape(1, c), ln_g.reshape(1, c), ln_b.reshape(1, c))


def _wkv_kernel(r_ref, d_ref, k_ref, v_ref, q_ref, p_ref, s0_ref, o_ref, st_ref, m_scr, *, round_state):
    c = pl.program_id(1)
    tc, hc, _ = r_ref.shape
    rnd = _bf16_round if round_state else (lambda z: z)

    @pl.when(c == 0)
    def _():
        m_scr[...] = s0_ref[...]

    def row(ref, t, k):
        return ref[t, k:k + 1, :]

    u0 = None
    for k in range(hc):
        term = rnd(m_scr[k]) * rnd(row(q_ref, 0, k))
        u0 = term if u0 is None else u0 + term

    def step(t, u):
        t_next = jnp.minimum(t + 1, tc - 1)
        v_t = v_ref[t]
        o = u_next = None
        for k in range(hc):
            m_k = m_scr[k] * row(d_ref, t, k) - u * row(p_ref, t, k) + v_t * row(k_ref, t, k)
            m_scr[k] = m_k
            m_b = rnd(m_k)
            t_o = m_b * rnd(row(r_ref, t, k))
            t_u = m_b * rnd(row(q_ref, t_next, k))
            o = t_o if o is None else o + t_o
            u_next = t_u if u_next is None else u_next + t_u
        o_ref[t] = o
        return u_next

    lax.fori_loop(0, tc, step, u0)

    @pl.when(c == pl.num_programs(1) - 1)
    def _():
        st_ref[...] = m_scr[...]


def _wkv(r, d, k, v, q, p, s0, *, hc, round_state):
    nseq, t, dm = r.shape
    n_heads = dm // hc
    n_prob = nseq * n_heads
    lanes = _tile(n_prob, LANES, 1)
    tc = _tile(t, 64, 1)
    to_lanes = lambda z: z.reshape(nseq, t, n_heads, hc).transpose(1, 3, 0, 2).reshape(t, hc, n_prob)
    seq_spec = pl.BlockSpec((tc, hc, lanes), lambda g, c: (c, 0, g))
    st_spec = pl.BlockSpec((hc, hc, lanes), lambda g, c: (0, 0, g))
    o, st = pl.pallas_call(
        functools.partial(_wkv_kernel, round_state=round_state),
        grid=(n_prob // lanes, t // tc),
        in_specs=[seq_spec] * 6 + [st_spec],
        out_specs=[seq_spec, st_spec],
        out_shape=[jax.ShapeDtypeStruct((t, hc, n_prob), F32), jax.ShapeDtypeStruct((hc, hc, n_prob), F32)],
        scratch_shapes=[pltpu.VMEM((hc, hc, lanes), F32)],
        compiler_params=_cparams("arbitrary", "arbitrary"),
    )(*[to_lanes(z) for z in (r, d, k, v, q, p)], s0.transpose(3, 2, 0, 1).reshape(hc, hc, n_prob))
    o = o.reshape(t, hc, nseq, n_heads).transpose(2, 0, 3, 1).reshape(nseq, t, dm)
    st = st.reshape(hc, hc, nseq, n_heads).transpose(2, 3, 1, 0)
    return o, st


def _router_kernel(x_ref, w_ref, idx_ref, gate_ref, *, n_exp):
    logits = jnp.dot(x_ref[...].astype(BF16), w_ref[...].astype(BF16), preferred_element_type=F32)
    lane_i = lax.broadcasted_iota(jnp.int32, logits.shape, 1)
    lane = lane_i.astype(F32)
    logits = jnp.where(lane_i < n_exp, logits, NEG_INF)
    m1 = jnp.max(logits, axis=-1, keepdims=True)
    i1 = jnp.min(jnp.where(logits == m1, lane, float(LANES)), axis=-1, keepdims=True)
    rest = jnp.where(lane == i1, NEG_INF, logits)
    m2 = jnp.max(rest, axis=-1, keepdims=True)
    i2 = jnp.min(jnp.where(rest == m2, lane, float(LANES)), axis=-1, keepdims=True)
    e2 = jnp.exp(m2 - m1)
    den = 1.0 + e2
    idx_ref[...] = jnp.where(lane_i == 0, i1, i2).astype(jnp.int32)
    gate_ref[...] = jnp.where(lane == 0, 1.0 / den, e2 / den)


def _router(x, w_router):
    m, d = x.shape
    n_exp = w_router.shape[1]
    w = jnp.pad(w_router, ((0, 0), (0, LANES - n_exp)))
    tm = _tile(m, 512, 8)
    out = pl.BlockSpec((tm, LANES), lambda i: (i, 0))
    idx, gate = pl.pallas_call(
        functools.partial(_router_kernel, n_exp=n_exp),
        grid=(m // tm,),
        in_specs=[pl.BlockSpec((tm, d), lambda i: (i, 0)), pl.BlockSpec((d, LANES), lambda i: (0, 0))],
        out_specs=[out, out],
        out_shape=[jax.ShapeDtypeStruct((m, LANES), jnp.int32), jax.ShapeDtypeStruct((m, LANES), F32)],
        compiler_params=_cparams("arbitrary"),
    )(x, w)
    return idx[:, :TOP_K], gate[:, :TOP_K]


def _gather_kernel(src_ref, x_hbm, o_ref, buf, sem):
    tr = o_ref.shape[0]
    base = pl.program_id(0) * tr

    def row_copy(i):
        return pltpu.make_async_copy(x_hbm.at[pl.ds(src_ref[base + i], 1)], buf.at[pl.ds(i, 1)], sem)

    def start(i, c):
        row_copy(i).start()
        return c

    def wait(i, c):
        row_copy(i).wait()
        return c

    lax.fori_loop(0, tr, start, 0)
    lax.fori_loop(0, tr, wait, 0)
    o_ref[...] = buf[...].astype(o_ref.dtype)


def _gather_rows(x, src, *, tr):
    rows = src.shape[0]
    d = x.shape[1]
    return pl.pallas_call(
        _gather_kernel,
        grid_spec=pltpu.PrefetchScalarGridSpec(
            num_scalar_prefetch=1,
            grid=(rows // tr,),
            in_specs=[pl.BlockSpec(memory_space=pl.ANY)],
            out_specs=pl.BlockSpec((tr, d), lambda i, src: (i, 0)),
            scratch_shapes=[pltpu.VMEM((tr, d), F32), pltpu.SemaphoreType.DMA(())],
        ),
        out_shape=jax.ShapeDtypeStruct((rows, d), BF16),
        compiler_params=_cparams("arbitrary"),
    )(src, x)


def _combine_ln_kernel(pos_ref, y_hbm, x_ref, gate_ref, g_ref, b_ref, o_ref, buf, sem, *, alpha):
    tr = o_ref.shape[0]
    base = pl.program_id(0) * tr

    def row_copy(i, slot):
        src = pos_ref[(base + i) * TOP_K + slot]
        return pltpu.make_async_copy(y_hbm.at[pl.ds(src, 1)], buf.at[slot, pl.ds(i, 1)], sem)

    def start(i, c):
        for slot in range(TOP_K):
            row_copy(i, slot).start()
        return c

    def wait(i, c):
        for slot in range(TOP_K):
            row_copy(i, slot).wait()
        return c

    lax.fori_loop(0, tr, start, 0)
    lax.fori_loop(0, tr, wait, 0)
    gates = gate_ref[...]
    f = gates[:, 0:1] * buf[0] + gates[:, 1:2] * buf[1]
    o_ref[...] = _ln_rows(alpha * x_ref[...] + f, g_ref[...], b_ref[...])


def _combine_ln(y_sorted, pos, gates, x, g, b, alpha):
    m, d = x.shape
    tr = _tile(m, 128, 8)
    row = pl.BlockSpec((tr, d), lambda i, pos: (i, 0))
    vec = pl.BlockSpec((1, d), lambda i, pos: (0, 0))
    return pl.pallas_call(
        functools.partial(_combine_ln_kernel, alpha=alpha),
        grid_spec=pltpu.PrefetchScalarGridSpec(
            num_scalar_prefetch=1,
            grid=(m // tr,),
            in_specs=[pl.BlockSpec(memory_space=pl.ANY), row,
                      pl.BlockSpec((tr, TOP_K), lambda i, pos: (i, 0)), vec, vec],
            out_specs=row,
            scratch_shapes=[pltpu.VMEM((TOP_K, tr, d), F32), pltpu.SemaphoreType.DMA(())],
        ),
        out_shape=jax.ShapeDtypeStruct((m, d), F32),
        compiler_params=_cparams("arbitrary"),
    )(pos, y_sorted, x, gates, g.reshape(1, d), b.reshape(1, d))


def _route_tables(top_i, n_exp, tm):
    m = top_i.shape[0]
    n_rows = m * TOP_K
    n_tiles = -(-(n_rows + n_exp * (tm - 1)) // tm)
    e_flat = top_i.reshape(-1)
    order = jnp.argsort(e_flat, stable=True)
    counts = jnp.sum(e_flat[:, None] == jnp.arange(n_exp, dtype=jnp.int32)[None, :], axis=0).astype(jnp.int32)
    padded = (counts + tm - 1) // tm * tm
    pad_end = jnp.cumsum(padded)
    pad_start = pad_end - padded
    cnt_start = jnp.cumsum(counts) - counts
    e_sorted = e_flat[order]
    dest_sorted = pad_start[e_sorted] + jnp.arange(n_rows, dtype=jnp.int32) - cnt_start[e_sorted]
    pos = jnp.zeros((n_rows,), jnp.int32).at[order].set(dest_sorted)
    src = jnp.zeros((n_tiles * tm,), jnp.int32).at[dest_sorted].set((order // TOP_K).astype(jnp.int32))
    n_valid = pad_end[-1] // tm
    t_idx = jnp.arange(n_tiles, dtype=jnp.int32)
    t_eff = jnp.minimum(t_idx, jnp.maximum(n_valid - 1, 0))
    te = jnp.searchsorted(pad_end, t_eff * tm, side="right").astype(jnp.int32)
    te = jnp.minimum(te, n_exp - 1)
    tf = jnp.concatenate([jnp.ones((1,), jnp.int32), (te[1:] != te[:-1]).astype(jnp.int32)])
    tv = (t_idx < n_valid).astype(jnp.int32)
    return src, pos, (te, tf, tv)


def _rope(x, pos, hd):
    half = hd // 2
    inv = jnp.power(ROPE_THETA, -jnp.arange(half, dtype=F32) / half)
    ang = pos.astype(F32)[:, None] * inv[None, :]
    cos, sin = jnp.cos(ang)[:, None, :], jnp.sin(ang)[:, None, :]
    xh = x.reshape(x.shape[0], -1, hd)
    x1, x2 = xh[..., :half], xh[..., half:]
    return jnp.concatenate([x1 * cos - x2 * sin, x2 * cos + x1 * sin], axis=-1).reshape(x.shape)


def _even_mixer(x_bf, dims, cache_k, cache_v, state_conv, w_in, b_in, sinks, conv_w, conv_b, conv_g, conv_bn,
                w_out, b_out):
    nseq, seq, db, ds, n_p = dims
    m = x_bf.shape[0]
    _, w, n_kv, hd = cache_k.shape
    d_kv = n_kv * hd
    d_q = sinks.shape[0] * hd
    d_conv = state_conv.shape[-1]
    cw = conv_w.shape[0]
    tm = _tile(m, 640, BF16_SUBLANES)
    h = _dense(x_bf, w_in, tm=tm, tn=_tile(w_in.shape[1], 512, LANES), bias=b_in)
    pos = jnp.concatenate([jnp.tile(jnp.arange(seq, dtype=jnp.int32), nseq),
                           jnp.tile(PAST_LEN + jnp.arange(ds, dtype=jnp.int32), db)])
    q = _rope(h[:, :d_q], pos, hd).astype(BF16)
    k = _rope(h[:, d_q:d_q + d_kv], pos, hd)
    v = h[:, d_q + d_kv:d_q + 2 * d_kv]
    a, gate = h[:, d_q + 2 * d_kv:d_q + 2 * d_kv + d_conv], h[:, d_q + 2 * d_kv + d_conv:]
    u = a * jax.nn.sigmoid(gate)
    k_bf, v_bf = k.astype(BF16), v.astype(BF16)

    att_p = _swa_prompt(q[:n_p], k_bf[:n_p], v_bf[:n_p], sinks, nseq=nseq, seq=seq, n_kv=n_kv, hd=hd, window=w)
    cache_k2 = cache_k.reshape(db, w, d_kv)
    cache_v2 = cache_v.reshape(db, w, d_kv)
    att_s = _swa_sample(q[n_p:].reshape(db, ds, d_q), k_bf[n_p:].reshape(db, ds, d_kv),
                        v_bf[n_p:].reshape(db, ds, d_kv), cache_k2, cache_v2, sinks, n_kv=n_kv, hd=hd)
    conv_p = _conv_prompt(u[:n_p], conv_w, conv_b, conv_g, conv_bn, nseq=nseq, seq=seq)
    up_s = jnp.concatenate([state_conv, u[n_p:].reshape(db, ds, d_conv)], axis=1)
    conv_s = _conv_sample(up_s, conv_w, conv_b, conv_g, conv_bn, ds=ds)

    mix = jnp.concatenate([jnp.concatenate([att_p, att_s.reshape(db * ds, d_q)], axis=0),
                           jnp.concatenate([conv_p, conv_s.reshape(db * ds, d_conv)], axis=0)], axis=1)
    y = _dense(mix, w_out, tm=tm, tn=_tile(w_out.shape[1], 512, LANES), bias=b_out)

    kp = k[:n_p].reshape(nseq, seq, n_kv, hd)[:, -w:]
    vp = v[:n_p].reshape(nseq, seq, n_kv, hd)[:, -w:]
    ks = jnp.concatenate([cache_k, k[n_p:].reshape(db, ds, n_kv, hd)], axis=1)[:, -w:]
    vs = jnp.concatenate([cache_v, v[n_p:].reshape(db, ds, n_kv, hd)], axis=1)[:, -w:]
    cp = u[:n_p].reshape(nseq, seq, d_conv)[:, -(cw - 1):]
    cs = up_s[:, -(cw - 1):]
    return y, (kp, vp, ks, vs, cp, cs)


def _rwkv_mixer(x, dims, state_shift, state_wkv, mu, w0, w1, w2, a0, a1, a2, g1, g2, k_k, k_a, r_k, w_r, w_k, w_v,
                w_o, lnx_g, lnx_b):
    nseq, seq, db, ds, n_p = dims
    m, d = x.shape
    n_heads, hc = state_wkv.shape[1], state_wkv.shape[-1]
    xp = x[:n_p].reshape(nseq, seq, d)
    xs = x[n_p:].reshape(db, ds, d)
    prev_p = jnp.concatenate([jnp.zeros((nseq, 1, d), F32), xp[:, :-1]], axis=1)
    prev_s = jnp.concatenate([state_shift[:, None], xs[:, :-1]], axis=1)
    prev = jnp.concatenate([prev_p.reshape(n_p, d), prev_s.reshape(db * ds, d)], axis=0)
    xx = prev - x
    xr, xw, xk, xv, xa, xg = [(x + xx * mu[i]).astype(BF16) for i in range(6)]
    tm = _tile(m, 640, BF16_SUBLANES)
    big = lambda z, wt: _dense(z, wt, tm=tm, tn=_tile(wt.shape[1], 512, LANES))
    r = big(xr, w_r)
    k = big(xk, w_k)
    v = big(xv, w_v)
    lw = big(jnp.tanh(big(xw, w1)).astype(BF16), w2)
    la = big(big(xa, a1).astype(BF16), a2)
    g = big(jax.nn.sigmoid(big(xg, g1)).astype(BF16), g2)
    w = -jax.nn.softplus(-(w0 + lw)) - 0.5
    a = jax.nn.sigmoid(a0 + la)
    heads = lambda z: z.reshape(m, n_heads, hc)
    kk = heads(k * k_k)
    kk = kk * lax.rsqrt(jnp.maximum(jnp.sum(kk * kk, axis=-1, keepdims=True), 1e-24))
    k = k * (1 + (a - 1) * k_a)
    decay = jnp.exp(-jnp.exp(w))
    kk = kk.reshape(m, d)
    p = kk * a

    seqs = lambda z, sl, shape: z[sl].reshape(shape)
    wkv_in = (r, decay, k, v, kk, p)
    args_p = [seqs(z, slice(0, n_p), (nseq, seq, d)) for z in wkv_in]
    args_s = [seqs(z, slice(n_p, m), (db, ds, d)) for z in wkv_in]
    o_p, wkv_p = _wkv(*args_p, jnp.zeros((nseq, n_heads, hc, hc), F32), hc=hc, round_state=False)
    o_s, wkv_s = _wkv(*args_s, state_wkv, hc=hc, round_state=True)
    o = heads(jnp.concatenate([o_p.reshape(n_p, d), o_s.reshape(db * ds, d)], axis=0))

    o_mean = jnp.mean(o, axis=-1, keepdims=True)
    o_var = jnp.mean(jnp.square(o - o_mean), axis=-1, keepdims=True)
    o = ((o - o_mean) * lax.rsqrt(o_var + GN_EPS)).reshape(m, d) * lnx_g + lnx_b
    bonus = jnp.sum(heads(r) * heads(k) * r_k, axis=-1, keepdims=True) * heads(v)
    o = o + bonus.reshape(m, d)
    y = big((o * g).astype(BF16), w_o)
    return y, (xp[:, -1], xs[:, -1], wkv_p, wkv_s)


def _swiglu_dense(x_bf, wg, wu, wd, *, tm):
    d_ff = wg.shape[1]
    h = _mm(x_bf, [wg[None], wu[None]], tm=tm, tn=_tile(d_ff, 256, LANES), out_dtype=BF16)
    nk = 2 if d_ff % (2 * LANES) == 0 and d_ff > 8192 else 1
    tm_d = _tile(x_bf.shape[0], 320, BF16_SUBLANES) if nk > 1 else tm
    return _dense(h, wd, tm=tm_d, tn=_tile(wd.shape[1], 512, LANES), nk=nk)


def _moe(x, x_ln_args, w_router, wg, wu, wd, alpha):
    m, d = x.shape
    n_exp, _, d_ff = wg.shape
    tm = 512 if m * TOP_K >= 4096 else 64
    top_i, gates = _router(x, w_router)
    src, pos, tiles = _route_tables(top_i, n_exp, tm)
    xs = _gather_rows(x, src, tr=_tile(src.shape[0], 256, BF16_SUBLANES))
    h = _mm(xs, [wg, wu], tm=tm, tn=_tile(d_ff, 256, LANES), out_dtype=BF16, tiles=tiles)
    nk = 4 if d_ff % (4 * LANES) == 0 and d_ff > 8192 else 1
    y = None
    for kk in range(nk):
        y = _mm(h, [wd], tm=tm, tn=_tile(d, 512, LANES), nk=nk, kk=kk, add=y, tiles=tiles)
    g, b = x_ln_args
    return _combine_ln(y, pos, gates, x, g, b, alpha)


def kernel(x_prompt, x_sample, cache_swa_k, cache_swa_v, state_conv, state_shift, state_wkv, ln_g, ln_b, w_in, b_in,
           sinks, conv_w, conv_b, conv_ln_g, conv_ln_b, w_out, b_out, ffn_w_gate, ffn_w_up, ffn_w_down, mu, w0, w1,
           w2, a0, a1, a2, g1, g2, k_k, k_a, r_k, w_r, w_k, w_v, w_o, lnx_g, lnx_b, w_router, moe_w_gate, moe_w_up,
           moe_w_down):
    nseq, seq, d = x_prompt.shape
    db, ds, _ = x_sample.shape
    n_p = nseq * seq
    dims = (nseq, seq, db, ds, n_p)
    depth = ln_g.shape[0]
    alpha = (2 * depth) ** 0.25
    x = jnp.concatenate([x_prompt.reshape(n_p, d), x_sample.reshape(db * ds, d)], axis=0)
    m = x.shape[0]
    tm = _tile(m, 640, BF16_SUBLANES)
    x_bf = x.astype(BF16)
    even_out, odd_out = [], []
    for layer in range(depth):
        i = layer // 2
        if layer % 2 == 0:
            y, states = _even_mixer(x_bf, dims, cache_swa_k[i], cache_swa_v[i], state_conv[i], w_in[i], b_in[i],
                                    sinks[i], conv_w[i], conv_b[i], conv_ln_g[i], conv_ln_b[i], w_out[i], b_out[i])
            even_out.append(states)
        else:
            y, states = _rwkv_mixer(x, dims, state_shift[i], state_wkv[i], mu[i], w0[i], w1[i], w2[i], a0[i], a1[i],
                                    a2[i], g1[i], g2[i], k_k[i], k_a[i], r_k[i], w_r[i], w_k[i], w_v[i], w_o[i],
                                    lnx_g[i], lnx_b[i])
            odd_out.append(states)
        x, x_bf = _add_ln(x, y, ln_g[layer, 0], ln_b[layer, 0], alpha)
        if layer % 2 == 0:
            f = _swiglu_dense(x_bf, ffn_w_gate[i], ffn_w_up[i], ffn_w_down[i], tm=tm)
            x, x_bf = _add_ln(x, f, ln_g[layer, 1], ln_b[layer, 1], alpha)
        else:
            x = _moe(x, (ln_g[layer, 1], ln_b[layer, 1]), w_router[i], moe_w_gate[i], moe_w_up[i], moe_w_down[i],
                     alpha)
            x_bf = x.astype(BF16)
    stack = lambda outs, j: jnp.stack([o[j] for o in outs])
    return (x[:n_p].reshape(nseq, seq, d), x[n_p:].reshape(db, ds, d),
            stack(even_out, 0), stack(even_out, 1), stack(even_out, 2), stack(even_out, 3),
            stack(even_out, 4), stack(even_out, 5),
            stack(odd_out, 0), stack(odd_out, 1), stack(odd_out, 2), stack(odd_out, 3))
```

```python
import functools
import math

import jax
import jax.numpy as jnp
from jax import lax
from jax.experimental import pallas as pl
from jax.experimental.pallas import tpu as pltpu

F32 = jnp.float32
BF16 = jnp.bfloat16

PAST_LEN = 16384
ROPE_THETA = 10000.0
LN_EPS = 1e-5
GN_EPS = 64e-5
TOP_K = 2
NEG_INF = float("-inf")

VMEM_LIMIT_BYTES = 58 * 1024 * 1024
LANES = 128
BF16_SUBLANES = 16


def _cparams(*sem):
    return pltpu.CompilerParams(dimension_semantics=sem, vmem_limit_bytes=VMEM_LIMIT_BYTES)


def _tile(n, target, mult):
    best = None
    for t in range(mult, min(n, target) + 1, mult):
        if n % t == 0:
            best = t
    return best if best is not None else n


def _mm_kernel(te_ref, tf_ref, tv_ref, tx_ref, to_ref, *refs, n_w, n_bias, has_add, act):
    del te_ref, tx_ref, to_ref
    x_ref = refs[0]
    w_refs = refs[1:1 + n_w]
    b_refs = refs[1 + n_w:1 + n_w + n_bias]
    pos = 1 + n_w + n_bias
    a_ref = None
    if has_add:
        a_ref = refs[pos]
        pos += 1
    o_ref = refs[pos]
    wb_refs = refs[pos + 1:pos + 1 + n_w]
    t = pl.program_id(1)

    @pl.when(tf_ref[t] == 1)
    def _():
        for w_ref, wb_ref in zip(w_refs, wb_refs):
            wb_ref[...] = w_ref[0].astype(BF16)

    @pl.when(tv_ref[t] == 1)
    def _():
        x = x_ref[...]
        accs = [jnp.dot(x, wb_ref[...], preferred_element_type=F32) for wb_ref in wb_refs]
        if n_bias:
            accs = [acc + b_ref[...] for acc, b_ref in zip(accs, b_refs)]
        if act == "swiglu":
            acc = accs[0] * jax.nn.sigmoid(accs[0]) * accs[1]
        elif act == "glu":
            acc = accs[0] * jax.nn.sigmoid(accs[1])
        else:
            acc = accs[0]
        if has_add:
            acc = acc + a_ref[...]
        o_ref[...] = acc.astype(o_ref.dtype)

    @pl.when(tv_ref[t] == 0)
    def _():
        o_ref[...] = jnp.zeros(o_ref.shape, o_ref.dtype)


def _mm(x, ws, *, tm, tn, nk=1, kk=0, cols=None, n_out=None, biases=None, add=None, act=None, out_dtype=F32,
        tiles=None):
    m, k = x.shape
    n = ws[0].shape[-1]
    n_out = n if n_out is None else n_out
    cols = (0,) * len(ws) if cols is None else cols
    assert m % tm == 0 and n_out % tn == 0 and k % nk == 0 and all(c % tn == 0 for c in cols)
    tk = k // nk
    nt = m // tm
    if tiles is None:
        idx = jnp.arange(nt, dtype=jnp.int32)
        tiles = (jnp.zeros((nt,), jnp.int32), (idx == 0).astype(jnp.int32), jnp.ones((nt,), jnp.int32), idx, idx)
    in_specs = [pl.BlockSpec((tm, tk), lambda j, t, te, tf, tv, tx, to: (tx[t], kk))]
    args = [x]
    for w, c in zip(ws, cols):
        in_specs.append(pl.BlockSpec((1, tk, tn), lambda j, t, te, tf, tv, tx, to, c=c: (te[t], kk, c // tn + j)))
        args.append(w)
    for b, c in zip(biases or (), cols):
        in_specs.append(pl.BlockSpec((1, tn), lambda j, t, te, tf, tv, tx, to, c=c: (0, c // tn + j)))
        args.append(b.reshape(1, n))
    out_spec = pl.BlockSpec((tm, tn), lambda j, t, te, tf, tv, tx, to: (to[t], j))
    if add is not None:
        in_specs.append(out_spec)
        args.append(add)
    kern = functools.partial(_mm_kernel, n_w=len(ws), n_bias=len(biases or ()), has_add=add is not None, act=act)
    return pl.pallas_call(
        kern,
        grid_spec=pltpu.PrefetchScalarGridSpec(
            num_scalar_prefetch=5,
            grid=(n_out // tn, nt),
            in_specs=in_specs,
            out_specs=out_spec,
            scratch_shapes=[pltpu.VMEM((tk, tn), BF16) for _ in ws],
        ),
        out_shape=jax.ShapeDtypeStruct((m, n_out), out_dtype),
        compiler_params=_cparams("arbitrary", "arbitrary"),
    )(*tiles, *args)


def _dense(x, w, *, tm, tn, nk=1, bias=None, col=0, n_out=None, out_dtype=F32):
    out = None
    for kk in range(nk):
        last = kk == nk - 1
        out = _mm(x, [w[None]], tm=tm, tn=tn, nk=nk, kk=kk, cols=(col,), n_out=n_out,
                  biases=[bias] if bias is not None and kk == 0 else None, add=out,
                  out_dtype=out_dtype if last else F32)
    return out


def _ln_rows(z, g, b):
    mean = jnp.mean(z, axis=-1, keepdims=True)
    zc = z - mean
    var = jnp.mean(zc * zc, axis=-1, keepdims=True)
    return zc * lax.rsqrt(var + LN_EPS) * g + b


def _add_ln_kernel(x_ref, y_ref, g_ref, b_ref, o_ref, ob_ref, *, alpha):
    out = _ln_rows(alpha * x_ref[...] + y_ref[...], g_ref[...], b_ref[...])
    o_ref[...] = out
    ob_ref[...] = out.astype(BF16)


def _add_ln(x, y, g, b, alpha):
    m, d = x.shape
    tm = _tile(m, 256, BF16_SUBLANES)
    row = pl.BlockSpec((tm, d), lambda i: (i, 0))
    vec = pl.BlockSpec((1, d), lambda i: (0, 0))
    return pl.pallas_call(
        functools.partial(_add_ln_kernel, alpha=alpha),
        grid=(m // tm,),
        in_specs=[row, row, vec, vec],
        out_specs=[row, row],
        out_shape=[jax.ShapeDtypeStruct((m, d), F32), jax.ShapeDtypeStruct((m, d), BF16)],
        compiler_params=_cparams("arbitrary"),
    )(x, y, g.reshape(1, d), b.reshape(1, d))


def _sink_softmax_pv(pieces, sink):
    m = sink
    for s, _ in pieces:
        m = jnp.maximum(jnp.max(s, axis=-1, keepdims=True), m)
    ps = [jnp.exp(s - m) for s, _ in pieces]
    den = jnp.exp(sink - m)
    for p in ps:
        den = den + jnp.sum(p, axis=-1, keepdims=True)
    out = None
    for p, (_, v) in zip(ps, pieces):
        o = jnp.dot((p / den).astype(BF16), v, preferred_element_type=F32)
        out = o if out is None else out + o
    return out


def _qk(q, k):
    return lax.dot_general(q, k, (((1,), (1,)), ((), ())), preferred_element_type=F32)


def _swa_prompt_kernel(sink_ref, q_ref, kp_ref, kc_ref, vp_ref, vc_ref, o_ref, *, n_kv, group, hd, window):
    i = pl.program_id(1)
    blk = q_ref.shape[0]
    kcat = jnp.concatenate([kp_ref[...], kc_ref[...]], axis=0)
    vcat = jnp.concatenate([vp_ref[...], vc_ref[...]], axis=0)
    row = lax.broadcasted_iota(jnp.int32, (blk, 2 * blk), 0)
    col = lax.broadcasted_iota(jnp.int32, (blk, 2 * blk), 1)
    diff = row + blk - col
    mask = (diff >= 0) & (diff <= window) & ((col >= blk) | (i > 0))
    scale = hd ** -0.5
    for g in range(n_kv):
        kg = kcat[:, g * hd:(g + 1) * hd]
        vg = vcat[:, g * hd:(g + 1) * hd]
        for j in range(group):
            h = g * group + j
            s = _qk(q_ref[:, h * hd:(h + 1) * hd], kg) * scale
            s = jnp.where(mask, s, NEG_INF)
            o = _sink_softmax_pv([(s, vg)], sink_ref[h])
            o_ref[:, h * hd:(h + 1) * hd] = o.astype(o_ref.dtype)


def _swa_prompt(q, k, v, sinks, *, nseq, seq, n_kv, hd, window):
    blk = window
    nb = seq // blk
    d_q, d_kv = q.shape[1], k.shape[1]
    group = d_q // hd // n_kv
    cur = lambda b, i: (b * nb + i, 0)
    prev = lambda b, i: (b * nb + jnp.maximum(i - 1, 0), 0)
    return pl.pallas_call(
        functools.partial(_swa_prompt_kernel, n_kv=n_kv, group=group, hd=hd, window=window),
        grid=(nseq, nb),
        in_specs=[pl.BlockSpec(memory_space=pltpu.SMEM),
                  pl.BlockSpec((blk, d_q), cur),
                  pl.BlockSpec((blk, d_kv), prev), pl.BlockSpec((blk, d_kv), cur),
                  pl.BlockSpec((blk, d_kv), prev), pl.BlockSpec((blk, d_kv), cur)],
        out_specs=pl.BlockSpec((blk, d_q), cur),
        out_shape=jax.ShapeDtypeStruct(q.shape, BF16),
        compiler_params=_cparams("arbitrary", "arbitrary"),
    )(sinks, q, k, k, v, v)


def _swa_sample_kernel(sink_ref, q_ref, kn_ref, vn_ref, kc_ref, vc_ref, o_ref, *, n_kv, group, hd):
    ds = kn_ref.shape[1]
    w = kc_ref.shape[1]
    rows = group * ds
    kc = kc_ref[0].astype(BF16)
    vc = vc_ref[0].astype(BF16)
    kn = kn_ref[0]
    vn = vn_ref[0]
    row = lax.broadcasted_iota(jnp.int32, (rows, 1), 0)
    q_head = jnp.zeros((rows, 1), jnp.int32)
    for j in range(1, group):
        q_head = jnp.where(row >= j * ds, j, q_head)
    q_pos = row - q_head * ds
    mask_c = lax.broadcasted_iota(jnp.int32, (rows, w), 1) >= q_pos
    mask_n = lax.broadcasted_iota(jnp.int32, (rows, ds), 1) <= q_pos
    scale = hd ** -0.5
    for g in range(n_kv):
        sl = slice(g * hd, (g + 1) * hd)
        sink = jnp.zeros((rows, 1), F32)
        for j in range(group):
            sink = jnp.where(q_head == j, sink_ref[g * group + j], sink)
        qg = q_ref[0, g]
        s_c = jnp.where(mask_c, _qk(qg, kc[:, sl]) * scale, NEG_INF)
        s_n = jnp.where(mask_n, _qk(qg, kn[:, sl]) * scale, NEG_INF)
        o = _sink_softmax_pv([(s_c, vc[:, sl]), (s_n, vn[:, sl])], sink)
        o_ref[0, g] = o.astype(o_ref.dtype)


def _swa_sample(q, kn, vn, kc, vc, sinks, *, n_kv, hd):
    db, ds, d_q = q.shape
    d_kv = kn.shape[2]
    w = kc.shape[1]
    group = d_q // hd // n_kv
    qg = q.reshape(db, ds, n_kv, group, hd).transpose(0, 2, 3, 1, 4).reshape(db, n_kv, group * ds, hd)
    b3 = lambda shape: pl.BlockSpec((1,) + shape, lambda b: (b, 0, 0))
    b4 = pl.BlockSpec((1, n_kv, group * ds, hd), lambda b: (b, 0, 0, 0))
    og = pl.pallas_call(
        functools.partial(_swa_sample_kernel, n_kv=n_kv, group=group, hd=hd),
        grid=(db,),
        in_specs=[pl.BlockSpec(memory_space=pltpu.SMEM), b4, b3((ds, d_kv)), b3((ds, d_kv)),
                  b3((w, d_kv)), b3((w, d_kv))],
        out_specs=b4,
        out_shape=jax.ShapeDtypeStruct(qg.shape, BF16),
        compiler_params=_cparams("arbitrary"),
    )(sinks, qg, kn, vn, kc, vc)
    return og.reshape(db, n_kv, group, ds, hd).transpose(0, 3, 1, 2, 4).reshape(db, ds, d_q)


def _bf16_round(x):
    return x.astype(BF16).astype(F32)


def _conv_tail(y, cb_ref, g_ref, b_ref):
    z = _ln_rows(y + cb_ref[...], g_ref[...], b_ref[...])
    return z * jax.nn.sigmoid(z)


def _conv_prompt_kernel(up_ref, uc_ref, w_ref, cb_ref, g_ref, b_ref, o_ref, *, conv_w, halo):
    i = pl.program_id(1)
    tq = uc_ref.shape[0]
    prev = jnp.where(i > 0, up_ref[tq - halo:, :], 0.0)
    up = _bf16_round(jnp.concatenate([prev, uc_ref[...]], axis=0))
    w = w_ref[...]
    off = halo - (conv_w - 1)
    y = None
    for j in range(conv_w):
        term = up[off + j:off + j + tq, :] * w[j:j + 1, :]
        y = term if y is None else y + term
    o_ref[...] = _conv_tail(y, cb_ref, g_ref, b_ref).astype(o_ref.dtype)


def _conv_prompt(u, conv_w, conv_b, ln_g, ln_b, *, nseq, seq):
    c = u.shape[1]
    cw = conv_w.shape[0]
    halo = -(-(cw - 1) // 8) * 8
    tq = _tile(seq, 256, BF16_SUBLANES)
    nb = seq // tq
    cur = lambda b, i: (b * nb + i, 0)
    prev = lambda b, i: (b * nb + jnp.maximum(i - 1, 0), 0)
    vec = pl.BlockSpec((1, c), lambda b, i: (0, 0))
    return pl.pallas_call(
        functools.partial(_conv_prompt_kernel, conv_w=cw, halo=halo),
        grid=(nseq, nb),
        in_specs=[pl.BlockSpec((tq, c), prev), pl.BlockSpec((tq, c), cur),
                  pl.BlockSpec((cw, c), lambda b, i: (0, 0)), vec, vec, vec],
        out_specs=pl.BlockSpec((tq, c), cur),
        out_shape=jax.ShapeDtypeStruct(u.shape, BF16),
        compiler_params=_cparams("arbitrary", "arbitrary"),
    )(u, u, conv_w, conv_b.reshape(1, c), ln_g.reshape(1, c), ln_b.reshape(1, c))


def _conv_sample_kernel(up_ref, w_ref, cb_ref, g_ref, b_ref, o_ref, *, conv_w):
    ds = o_ref.shape[1]
    up = _bf16_round(up_ref[0])
    w = w_ref[...]
    y = None
    for j in range(conv_w):
        term = up[j:j + ds, :] * w[j:j + 1, :]
        y = term if y is None else y + term
    o_ref[0] = _conv_tail(y, cb_ref, g_ref, b_ref).astype(o_ref.dtype)


def _conv_sample(up, conv_w, conv_b, ln_g, ln_b, *, ds):
    db, length, c = up.shape
    cw = conv_w.shape[0]
    vec = pl.BlockSpec((1, c), lambda b: (0, 0))
    return pl.pallas_call(
        functools.partial(_conv_sample_kernel, conv_w=cw),
        grid=(db,),
        in_specs=[pl.BlockSpec((1, length, c), lambda b: (b, 0, 0)), pl.BlockSpec((cw, c), lambda b: (0, 0)),
                  vec, vec, vec],
        out_specs=pl.BlockSpec((1, ds, c), lambda b: (b, 0, 0)),
        out_shape=jax.ShapeDtypeStruct((db, ds, c), BF16),
        compiler_params=_cparams("arbitrary"),
    )(up, conv_w, conv_b.reshape(1, c), ln_g.reshape(1, c), ln_b.reshape(1, c))


def _wkv_kernel(r_ref, d_ref, k_ref, v_ref, q_ref, p_ref, s0_ref, o_ref, st_ref, m_scr, *, round_state):
    c = pl.program_id(1)
    tc, hc, _ = r_ref.shape
    rnd = _bf16_round if round_state else (lambda z: z)

    @pl.when(c == 0)
    def _():
        m_scr[...] = s0_ref[...]

    def row(ref, t, k):
        return ref[t, k:k + 1, :]

    u0 = None
    for k in range(hc):
        term = rnd(m_scr[k]) * rnd(row(q_ref, 0, k))
        u0 = term if u0 is None else u0 + term

    def step(t, u):
        t_next = jnp.minimum(t + 1, tc - 1)
        v_t = v_ref[t]
        o = u_next = None
        for k in range(hc):
            m_k = m_scr[k] * row(d_ref, t, k) - u * row(p_ref, t, k) + v_t * row(k_ref, t, k)
            m_scr[k] = m_k
            m_b = rnd(m_k)
            t_o = m_b * rnd(row(r_ref, t, k))
            t_u = m_b * rnd(row(q_ref, t_next, k))
            o = t_o if o is None else o + t_o
            u_next = t_u if u_next is None else u_next + t_u
        o_ref[t] = o
        return u_next

    lax.fori_loop(0, tc, step, u0)

    @pl.when(c == pl.num_programs(1) - 1)
    def _():
        st_ref[...] = m_scr[...]


def _wkv(r, d, k, v, q, p, s0, *, hc, round_state):
    nseq, t, dm = r.shape
    n_heads = dm // hc
    n_prob = nseq * n_heads
    lanes = _tile(n_prob, LANES, 1)
    tc = _tile(t, 64, 1)
    to_lanes = lambda z: z.reshape(nseq, t, n_heads, hc).transpose(1, 3, 0, 2).reshape(t, hc, n_prob)
    seq_spec = pl.BlockSpec((tc, hc, lanes), lambda g, c: (c, 0, g))
    st_spec = pl.BlockSpec((hc, hc, lanes), lambda g, c: (0, 0, g))
    o, st = pl.pallas_call(
        functools.partial(_wkv_kernel, round_state=round_state),
        grid=(n_prob // lanes, t // tc),
        in_specs=[seq_spec] * 6 + [st_spec],
        out_specs=[seq_spec, st_spec],
        out_shape=[jax.ShapeDtypeStruct((t, hc, n_prob), F32), jax.ShapeDtypeStruct((hc, hc, n_prob), F32)],
        scratch_shapes=[pltpu.VMEM((hc, hc, lanes), F32)],
        compiler_params=_cparams("arbitrary", "arbitrary"),
    )(*[to_lanes(z) for z in (r, d, k, v, q, p)], s0.transpose(3, 2, 0, 1).reshape(hc, hc, n_prob))
    o = o.reshape(t, hc, nseq, n_heads).transpose(2, 0, 3, 1).reshape(nseq, t, dm)
    st = st.reshape(hc, hc, nseq, n_heads).transpose(2, 3, 1, 0)
    return o, st


def _router_kernel(x_ref, w_ref, idx_ref, gate_ref, *, n_exp):
    logits = jnp.dot(x_ref[...].astype(BF16), w_ref[...].astype(BF16), preferred_element_type=F32)
    lane_i = lax.broadcasted_iota(jnp.int32, logits.shape, 1)
    lane = lane_i.astype(F32)
    logits = jnp.where(lane_i < n_exp, logits, NEG_INF)
    m1 = jnp.max(logits, axis=-1, keepdims=True)
    i1 = jnp.min(jnp.where(logits == m1, lane, float(LANES)), axis=-1, keepdims=True)
    rest = jnp.where(lane == i1, NEG_INF, logits)
    m2 = jnp.max(rest, axis=-1, keepdims=True)
    i2 = jnp.min(jnp.where(rest == m2, lane, float(LANES)), axis=-1, keepdims=True)
    e2 = jnp.exp(m2 - m1)
    den = 1.0 + e2
    idx_ref[...] = jnp.where(lane_i == 0, i1, i2).astype(jnp.int32)
    gate_ref[...] = jnp.where(lane == 0, 1.0 / den, e2 / den)


def _router(x, w_router):
    m, d = x.shape
    n_exp = w_router.shape[1]
    w = jnp.pad(w_router, ((0, 0), (0, LANES - n_exp)))
    tm = _tile(m, 512, 8)
    out = pl.BlockSpec((tm, LANES), lambda i: (i, 0))
    idx, gate = pl.pallas_call(
        functools.partial(_router_kernel, n_exp=n_exp),
        grid=(m // tm,),
        in_specs=[pl.BlockSpec((tm, d), lambda i: (i, 0)), pl.BlockSpec((d, LANES), lambda i: (0, 0))],
        out_specs=[out, out],
        out_shape=[jax.ShapeDtypeStruct((m, LANES), jnp.int32), jax.ShapeDtypeStruct((m, LANES), F32)],
        compiler_params=_cparams("arbitrary"),
    )(x, w)
    return idx[:, :TOP_K], gate[:, :TOP_K]


def _gather_kernel(src_ref, x_hbm, o_ref, buf, sem):
    tr = o_ref.shape[0]
    base = pl.program_id(0) * tr

    def row_copy(i):
        return pltpu.make_async_copy(x_hbm.at[pl.ds(src_ref[base + i], 1)], buf.at[pl.ds(i, 1)], sem)

    def start(i, c):
        row_copy(i).start()
        return c

    def wait(i, c):
        row_copy(i).wait()
        return c

    lax.fori_loop(0, tr, start, 0)
    lax.fori_loop(0, tr, wait, 0)
    o_ref[...] = buf[...].astype(o_ref.dtype)


def _gather_rows(x, src, *, tr):
    rows = src.shape[0]
    d = x.shape[1]
    return pl.pallas_call(
        _gather_kernel,
        grid_spec=pltpu.PrefetchScalarGridSpec(
            num_scalar_prefetch=1,
            grid=(rows // tr,),
            in_specs=[pl.BlockSpec(memory_space=pl.ANY)],
            out_specs=pl.BlockSpec((tr, d), lambda i, src: (i, 0)),
            scratch_shapes=[pltpu.VMEM((tr, d), F32), pltpu.SemaphoreType.DMA(())],
        ),
        out_shape=jax.ShapeDtypeStruct((rows, d), BF16),
        compiler_params=_cparams("arbitrary"),
    )(src, x)


def _combine_ln_kernel(pos_ref, y_hbm, x_ref, gate_ref, g_ref, b_ref, o_ref, buf, sem, *, alpha):
    tr = o_ref.shape[0]
    base = pl.program_id(0) * tr

    def row_copy(i, slot):
        src = pos_ref[(base + i) * TOP_K + slot]
        return pltpu.make_async_copy(y_hbm.at[pl.ds(src, 1)], buf.at[slot, pl.ds(i, 1)], sem)

    def start(i, c):
        for slot in range(TOP_K):
            row_copy(i, slot).start()
        return c

    def wait(i, c):
        for slot in range(TOP_K):
            row_copy(i, slot).wait()
        return c

    lax.fori_loop(0, tr, start, 0)
    lax.fori_loop(0, tr, wait, 0)
    gates = gate_ref[...]
    f = gates[:, 0:1] * buf[0] + gates[:, 1:2] * buf[1]
    o_ref[...] = _ln_rows(alpha * x_ref[...] + f, g_ref[...], b_ref[...])


def _combine_ln(y_sorted, pos, gates, x, g, b, alpha):
    m, d = x.shape
    tr = _tile(m, 128, 8)
    row = pl.BlockSpec((tr, d), lambda i, pos: (i, 0))
    vec = pl.BlockSpec((1, d), lambda i, pos: (0, 0))
    return pl.pallas_call(
        functools.partial(_combine_ln_kernel, alpha=alpha),
        grid_spec=pltpu.PrefetchScalarGridSpec(
            num_scalar_prefetch=1,
            grid=(m // tr,),
            in_specs=[pl.BlockSpec(memory_space=pl.ANY), row,
                      pl.BlockSpec((tr, TOP_K), lambda i, pos: (i, 0)), vec, vec],
            out_specs=row,
            scratch_shapes=[pltpu.VMEM((TOP_K, tr, d), F32), pltpu.SemaphoreType.DMA(())],
        ),
        out_shape=jax.ShapeDtypeStruct((m, d), F32),
        compiler_params=_cparams("arbitrary"),
    )(pos, y_sorted, x, gates, g.reshape(1, d), b.reshape(1, d))


def _route_tables(top_i, n_exp, tm):
    m = top_i.shape[0]
    n_rows = m * TOP_K
    n_tiles = -(-(n_rows + n_exp * (tm - 1)) // tm)
    e_flat = top_i.reshape(-1)
    order = jnp.argsort(e_flat, stable=True)
    counts = jnp.sum(e_flat[:, None] == jnp.arange(n_exp, dtype=jnp.int32)[None, :], axis=0).astype(jnp.int32)
    padded = (counts + tm - 1) // tm * tm
    pad_end = jnp.cumsum(padded)
    pad_start = pad_end - padded
    cnt_start = jnp.cumsum(counts) - counts
    e_sorted = e_flat[order]
    dest_sorted = pad_start[e_sorted] + jnp.arange(n_rows, dtype=jnp.int32) - cnt_start[e_sorted]
    pos = jnp.zeros((n_rows,), jnp.int32).at[order].set(dest_sorted)
    src = jnp.zeros((n_tiles * tm,), jnp.int32).at[dest_sorted].set((order // TOP_K).astype(jnp.int32))
    n_valid = pad_end[-1] // tm
    t_idx = jnp.arange(n_tiles, dtype=jnp.int32)
    t_eff = jnp.minimum(t_idx, jnp.maximum(n_valid - 1, 0))
    te = jnp.searchsorted(pad_end, t_eff * tm, side="right").astype(jnp.int32)
    te = jnp.minimum(te, n_exp - 1)
    tf = jnp.concatenate([jnp.ones((1,), jnp.int32), (te[1:] != te[:-1]).astype(jnp.int32)])
    tv = (t_idx < n_valid).astype(jnp.int32)
    return src, pos, (te, tf, tv, t_eff, t_idx)


def _rope(x, pos, hd):
    half = hd // 2
    inv = jnp.power(ROPE_THETA, -jnp.arange(half, dtype=F32) / half)
    ang = pos.astype(F32)[:, None] * inv[None, :]
    cos, sin = jnp.cos(ang)[:, None, :], jnp.sin(ang)[:, None, :]
    xh = x.reshape(x.shape[0], -1, hd)
    x1, x2 = xh[..., :half], xh[..., half:]
    return jnp.concatenate([x1 * cos - x2 * sin, x2 * cos + x1 * sin], axis=-1).reshape(x.shape)


def _even_mixer(x_bf, dims, cache_k, cache_v, state_conv, w_in, b_in, sinks, conv_w, conv_b, conv_g, conv_bn,
                w_out, b_out):
    nseq, seq, db, ds, n_p = dims
    m = x_bf.shape[0]
    _, w, n_kv, hd = cache_k.shape
    d_kv = n_kv * hd
    d_q = sinks.shape[0] * hd
    d_conv = state_conv.shape[-1]
    cw = conv_w.shape[0]
    tm = _tile(m, 640, BF16_SUBLANES)
    col_a, col_g = d_q + 2 * d_kv, d_q + 2 * d_kv + d_conv

    def proj(col, width):
        tn = _tile(math.gcd(col, width), 512, LANES)
        return _dense(x_bf, w_in, tm=tm, tn=tn, bias=b_in, col=col, n_out=width)

    pos = jnp.concatenate([jnp.tile(jnp.arange(seq, dtype=jnp.int32), nseq),
                           jnp.tile(PAST_LEN + jnp.arange(ds, dtype=jnp.int32), db)])
    q = _rope(proj(0, d_q), pos, hd).astype(BF16)
    if d_kv % LANES == 0:
        k, v = proj(d_q, d_kv), proj(d_q + d_kv, d_kv)
    else:
        kv = proj(d_q, col_a - d_q)
        k, v = kv[:, :d_kv], kv[:, d_kv:]
    k = _rope(k, pos, hd)
    u = _mm(x_bf, [w_in[None], w_in[None]], tm=tm, tn=_tile(math.gcd(col_a, col_g, d_conv), 256, LANES),
            cols=(col_a, col_g), n_out=d_conv, biases=[b_in, b_in], act="glu")
    k_bf, v_bf = k.astype(BF16), v.astype(BF16)

    att_p = _swa_prompt(q[:n_p], k_bf[:n_p], v_bf[:n_p], sinks, nseq=nseq, seq=seq, n_kv=n_kv, hd=hd, window=w)
    cache_k2 = cache_k.reshape(db, w, d_kv)
    cache_v2 = cache_v.reshape(db, w, d_kv)
    att_s = _swa_sample(q[n_p:].reshape(db, ds, d_q), k_bf[n_p:].reshape(db, ds, d_kv),
                        v_bf[n_p:].reshape(db, ds, d_kv), cache_k2, cache_v2, sinks, n_kv=n_kv, hd=hd)
    conv_p = _conv_prompt(u[:n_p], conv_w, conv_b, conv_g, conv_bn, nseq=nseq, seq=seq)
    up_s = jnp.concatenate([state_conv, u[n_p:].reshape(db, ds, d_conv)], axis=1)
    conv_s = _conv_sample(up_s, conv_w, conv_b, conv_g, conv_bn, ds=ds)

    mix = jnp.concatenate([jnp.concatenate([att_p, att_s.reshape(db * ds, d_q)], axis=0),
                           jnp.concatenate([conv_p, conv_s.reshape(db * ds, d_conv)], axis=0)], axis=1)
    y = _dense(mix, w_out, tm=tm, tn=_tile(w_out.shape[1], 512, LANES), bias=b_out)

    kp = k[:n_p].reshape(nseq, seq, n_kv, hd)[:, -w:]
    vp = v[:n_p].reshape(nseq, seq, n_kv, hd)[:, -w:]
    ks = jnp.concatenate([cache_k, k[n_p:].reshape(db, ds, n_kv, hd)], axis=1)[:, -w:]
    vs = jnp.concatenate([cache_v, v[n_p:].reshape(db, ds, n_kv, hd)], axis=1)[:, -w:]
    cp = u[:n_p].reshape(nseq, seq, d_conv)[:, -(cw - 1):]
    cs = up_s[:, -(cw - 1):]
    return y, (kp, vp, ks, vs, cp, cs)


def _rwkv_mixer(x, dims, state_shift, state_wkv, mu, w0, w1, w2, a0, a1, a2, g1, g2, k_k, k_a, r_k, w_r, w_k, w_v,
                w_o, lnx_g, lnx_b):
    nseq, seq, db, ds, n_p = dims
    m, d = x.shape
    n_heads, hc = state_wkv.shape[1], state_wkv.shape[-1]
    xp = x[:n_p].reshape(nseq, seq, d)
    xs = x[n_p:].reshape(db, ds, d)
    prev_p = jnp.concatenate([jnp.zeros((nseq, 1, d), F32), xp[:, :-1]], axis=1)
    prev_s = jnp.concatenate([state_shift[:, None], xs[:, :-1]], axis=1)
    prev = jnp.concatenate([prev_p.reshape(n_p, d), prev_s.reshape(db * ds, d)], axis=0)
    xx = prev - x
    xr, xw, xk, xv, xa, xg = [(x + xx * mu[i]).astype(BF16) for i in range(6)]
    tm = _tile(m, 640, BF16_SUBLANES)
    big = lambda z, wt: _dense(z, wt, tm=tm, tn=_tile(wt.shape[1], 512, LANES))
    r = big(xr, w_r)
    k = big(xk, w_k)
    v = big(xv, w_v)
    lw = big(jnp.tanh(big(xw, w1)).astype(BF16), w2)
    la = big(big(xa, a1).astype(BF16), a2)
    g = big(jax.nn.sigmoid(big(xg, g1)).astype(BF16), g2)

    def part(rows, n_s, t, state, round_state):
        r_, k_, v_, lw_, la_, g_ = [z[rows] for z in (r, k, v, lw, la, g)]
        n = n_s * t
        heads = lambda z: z.reshape(n, n_heads, hc)
        w = -jax.nn.softplus(-(w0 + lw_)) - 0.5
        a = jax.nn.sigmoid(a0 + la_)
        kk = heads(k_ * k_k)
        kk = (kk * lax.rsqrt(jnp.maximum(jnp.sum(kk * kk, axis=-1, keepdims=True), 1e-24))).reshape(n, d)
        k_ = k_ * (1 + (a - 1) * k_a)
        decay = jnp.exp(-jnp.exp(w))
        wkv_in = [z.reshape(n_s, t, d) for z in (r_, decay, k_, v_, kk, kk * a)]
        o, st = _wkv(*wkv_in, state, hc=hc, round_state=round_state)
        o = heads(o.reshape(n, d))
        o_mean = jnp.mean(o, axis=-1, keepdims=True)
        o_var = jnp.mean(jnp.square(o - o_mean), axis=-1, keepdims=True)
        o = ((o - o_mean) * lax.rsqrt(o_var + GN_EPS)).reshape(n, d) * lnx_g + lnx_b
        bonus = jnp.sum(heads(r_) * heads(k_) * r_k, axis=-1, keepdims=True) * heads(v_)
        return ((o + bonus.reshape(n, d)) * g_).astype(BF16), st

    og_p, wkv_p = part(slice(0, n_p), nseq, seq, jnp.zeros((nseq, n_heads, hc, hc), F32), False)
    og_s, wkv_s = part(slice(n_p, m), db, ds, state_wkv, True)
    y = big(jnp.concatenate([og_p, og_s], axis=0), w_o)
    return y, (xp[:, -1], xs[:, -1], wkv_p, wkv_s)


def _swiglu_dense(x_bf, wg, wu, wd, *, tm):
    d_ff = wg.shape[1]
    h = _mm(x_bf, [wg[None], wu[None]], tm=tm, tn=_tile(d_ff, 256, LANES), act="swiglu", out_dtype=BF16)
    nk = 2 if d_ff % (2 * LANES) == 0 and d_ff > 8192 else 1
    return _dense(h, wd, tm=tm, tn=_tile(wd.shape[1], 512, LANES), nk=nk)


def _moe(x, x_ln_args, w_router, wg, wu, wd, alpha):
    m, d = x.shape
    n_exp, _, d_ff = wg.shape
    tm = 512 if m * TOP_K >= 4096 else 64
    top_i, gates = _router(x, w_router)
    src, pos, tiles = _route_tables(top_i, n_exp, tm)
    xs = _gather_rows(x, src, tr=_tile(src.shape[0], 256, BF16_SUBLANES))
    h = _mm(xs, [wg, wu], tm=tm, tn=_tile(d_ff, 512, LANES), act="swiglu", out_dtype=BF16, tiles=tiles)
    nk = 4 if d_ff % (4 * LANES) == 0 and d_ff > 8192 else 1
    y = None
    for kk in range(nk):
        y = _mm(h, [wd], tm=tm, tn=_tile(d, 512, LANES), nk=nk, kk=kk, add=y, tiles=tiles)
    g, b = x_ln_args
    return _combine_ln(y, pos, gates, x, g, b, alpha)


def kernel(x_prompt, x_sample, cache_swa_k, cache_swa_v, state_conv, state_shift, state_wkv, ln_g, ln_b, w_in, b_in,
           sinks, conv_w, conv_b, conv_ln_g, conv_ln_b, w_out, b_out, ffn_w_gate, ffn_w_up, ffn_w_down, mu, w0, w1,
           w2, a0, a1, a2, g1, g2, k_k, k_a, r_k, w_r, w_k, w_v, w_o, lnx_g, lnx_b, w_router, moe_w_gate, moe_w_up,
           moe_w_down):
    nseq, seq, d = x_prompt.shape
    db, ds, _ = x_sample.shape
    n_p = nseq * seq
    dims = (nseq, seq, db, ds, n_p)
    depth = ln_g.shape[0]
    alpha = (2 * depth) ** 0.25
    x = jnp.concatenate([x_prompt.reshape(n_p, d), x_sample.reshape(db * ds, d)], axis=0)
    m = x.shape[0]
    tm = _tile(m, 640, BF16_SUBLANES)
    x_bf = x.astype(BF16)
    even_out, odd_out = [], []
    for layer in range(depth):
        i = layer // 2
        if layer % 2 == 0:
            y, states = _even_mixer(x_bf, dims, cache_swa_k[i], cache_swa_v[i], state_conv[i], w_in[i], b_in[i],
                                    sinks[i], conv_w[i], conv_b[i], conv_ln_g[i], conv_ln_b[i], w_out[i], b_out[i])
            even_out.append(states)
        else:
            y, states = _rwkv_mixer(x, dims, state_shift[i], state_wkv[i], mu[i], w0[i], w1[i], w2[i], a0[i], a1[i],
                                    a2[i], g1[i], g2[i], k_k[i], k_a[i], r_k[i], w_r[i], w_k[i], w_v[i], w_o[i],
                                    lnx_g[i], lnx_b[i])
            odd_out.append(states)
        x, x_bf = _add_ln(x, y, ln_g[layer, 0], ln_b[layer, 0], alpha)
        if layer % 2 == 0:
            f = _swiglu_dense(x_bf, ffn_w_gate[i], ffn_w_up[i], ffn_w_down[i], tm=tm)
            x, x_bf = _add_ln(x, f, ln_g[layer, 1], ln_b[layer, 1], alpha)
        else:
            x = _moe(x, (ln_g[layer, 1], ln_b[layer, 1]), w_router[i], moe_w_gate[i], moe_w_up[i], moe_w_down[i],
                     alpha)
            x_bf = x.astype(BF16)
    stack = lambda outs, j: jnp.stack([o[j] for o in outs])
    return (x[:n_p].reshape(nseq, seq, d), x[n_p:].reshape(db, ds, d),
            stack(even_out, 0), stack(even_out, 1), stack(even_out, 2), stack(even_out, 3),
            stack(even_out, 4), stack(even_out, 5),
            stack(odd_out, 0), stack(odd_out, 1), stack(odd_out, 2), stack(odd_out, 3))
```

```python
import functools
import math

import jax
import jax.numpy as jnp
from jax import lax
from jax.experimental import pallas as pl
from jax.experimental.pallas import tpu as pltpu

F32 = jnp.float32
BF16 = jnp.bfloat16

PAST_LEN = 16384
ROPE_THETA = 10000.0
LN_EPS = 1e-5
GN_EPS = 64e-5
TOP_K = 2
NEG_INF = float("-inf")

VMEM_LIMIT_BYTES = 58 * 1024 * 1024
LANES = 128
BF16_SUBLANES = 16


def _cparams(*sem):
    return pltpu.CompilerParams(dimension_semantics=sem, vmem_limit_bytes=VMEM_LIMIT_BYTES)


def _tile(n, target, mult):
    best = None
    for t in range(mult, min(n, target) + 1, mult):
        if n % t == 0:
            best = t
    return best if best is not None else n


def _mm_kernel(te_ref, tf_ref, tv_ref, tx_ref, to_ref, *refs, n_w, n_bias, has_add, act):
    del te_ref, tx_ref, to_ref
    x_ref = refs[0]
    w_refs = refs[1:1 + n_w]
    b_refs = refs[1 + n_w:1 + n_w + n_bias]
    pos = 1 + n_w + n_bias
    a_ref = None
    if has_add:
        a_ref = refs[pos]
        pos += 1
    o_ref = refs[pos]
    wb_refs = refs[pos + 1:pos + 1 + n_w]
    t = pl.program_id(1)

    @pl.when(tf_ref[t] == 1)
    def _():
        for w_ref, wb_ref in zip(w_refs, wb_refs):
            wb_ref[...] = w_ref[0].astype(BF16)

    @pl.when(tv_ref[t] == 1)
    def _():
        x = x_ref[...]
        accs = [jnp.dot(x, wb_ref[...], preferred_element_type=F32) for wb_ref in wb_refs]
        if n_bias:
            accs = [acc + b_ref[...] for acc, b_ref in zip(accs, b_refs)]
        if act == "swiglu":
            acc = accs[0] * jax.nn.sigmoid(accs[0]) * accs[1]
        elif act == "glu":
            acc = accs[0] * jax.nn.sigmoid(accs[1])
        else:
            acc = accs[0]
        if has_add:
            acc = acc + a_ref[...]
        o_ref[...] = acc.astype(o_ref.dtype)

    @pl.when(tv_ref[t] == 0)
    def _():
        o_ref[...] = jnp.zeros(o_ref.shape, o_ref.dtype)


def _mm(x, ws, *, tm, tn, nk=1, kk=0, cols=None, n_out=None, biases=None, add=None, act=None, out_dtype=F32,
        tiles=None):
    m, k = x.shape
    n = ws[0].shape[-1]
    n_out = n if n_out is None else n_out
    cols = (0,) * len(ws) if cols is None else cols
    assert m % tm == 0 and n_out % tn == 0 and k % nk == 0 and all(c % tn == 0 for c in cols)
    tk = k // nk
    nt = m // tm
    if tiles is None:
        idx = jnp.arange(nt, dtype=jnp.int32)
        tiles = (jnp.zeros((nt,), jnp.int32), (idx == 0).astype(jnp.int32), jnp.ones((nt,), jnp.int32), idx, idx)
    in_specs = [pl.BlockSpec((tm, tk), lambda j, t, te, tf, tv, tx, to: (tx[t], kk))]
    args = [x]
    for w, c in zip(ws, cols):
        in_specs.append(pl.BlockSpec((1, tk, tn), lambda j, t, te, tf, tv, tx, to, c=c: (te[t], kk, c // tn + j)))
        args.append(w)
    for b, c in zip(biases or (), cols):
        in_specs.append(pl.BlockSpec((1, tn), lambda j, t, te, tf, tv, tx, to, c=c: (0, c // tn + j)))
        args.append(b.reshape(1, n))
    out_spec = pl.BlockSpec((tm, tn), lambda j, t, te, tf, tv, tx, to: (to[t], j))
    if add is not None:
        in_specs.append(out_spec)
        args.append(add)
    kern = functools.partial(_mm_kernel, n_w=len(ws), n_bias=len(biases or ()), has_add=add is not None, act=act)
    return pl.pallas_call(
        kern,
        grid_spec=pltpu.PrefetchScalarGridSpec(
            num_scalar_prefetch=5,
            grid=(n_out // tn, nt),
            in_specs=in_specs,
            out_specs=out_spec,
            scratch_shapes=[pltpu.VMEM((tk, tn), BF16) for _ in ws],
        ),
        out_shape=jax.ShapeDtypeStruct((m, n_out), out_dtype),
        compiler_params=_cparams("arbitrary", "arbitrary"),
    )(*tiles, *args)


def _dense(x, w, *, tm, tn, nk=1, bias=None, col=0, n_out=None, out_dtype=F32):
    out = None
    for kk in range(nk):
        last = kk == nk - 1
        out = _mm(x, [w[None]], tm=tm, tn=tn, nk=nk, kk=kk, cols=(col,), n_out=n_out,
                  biases=[bias] if bias is not None and kk == 0 else None, add=out,
                  out_dtype=out_dtype if last else F32)
    return out


def _ln_rows(z, g, b):
    mean = jnp.mean(z, axis=-1, keepdims=True)
    zc = z - mean
    var = jnp.mean(zc * zc, axis=-1, keepdims=True)
    return zc * lax.rsqrt(var + LN_EPS) * g + b


def _add_ln_kernel(x_ref, y_ref, g_ref, b_ref, o_ref, ob_ref, *, alpha):
    out = _ln_rows(alpha * x_ref[...] + y_ref[...], g_ref[...], b_ref[...])
    o_ref[...] = out
    ob_ref[...] = out.astype(BF16)


def _add_ln(x, y, g, b, alpha):
    m, d = x.shape
    tm = _tile(m, 256, BF16_SUBLANES)
    row = pl.BlockSpec((tm, d), lambda i: (i, 0))
    vec = pl.BlockSpec((1, d), lambda i: (0, 0))
    return pl.pallas_call(
        functools.partial(_add_ln_kernel, alpha=alpha),
        grid=(m // tm,),
        in_specs=[row, row, vec, vec],
        out_specs=[row, row],
        out_shape=[jax.ShapeDtypeStruct((m, d), F32), jax.ShapeDtypeStruct((m, d), BF16)],
        compiler_params=_cparams("arbitrary"),
    )(x, y, g.reshape(1, d), b.reshape(1, d))


def _sink_softmax_pv(pieces, sink):
    m = sink
    for s, _ in pieces:
        m = jnp.maximum(jnp.max(s, axis=-1, keepdims=True), m)
    ps = [jnp.exp(s - m) for s, _ in pieces]
    den = jnp.exp(sink - m)
    for p in ps:
        den = den + jnp.sum(p, axis=-1, keepdims=True)
    out = None
    for p, (_, v) in zip(ps, pieces):
        o = jnp.dot((p / den).astype(BF16), v, preferred_element_type=F32)
        out = o if out is None else out + o
    return out


def _qk(q, k):
    return lax.dot_general(q, k, (((1,), (1,)), ((), ())), preferred_element_type=F32)


def _swa_prompt_kernel(sink_ref, q_ref, kp_ref, kc_ref, vp_ref, vc_ref, o_ref, *, n_kv, group, hd, window):
    i = pl.program_id(1)
    blk = q_ref.shape[0]
    kcat = jnp.concatenate([kp_ref[...], kc_ref[...]], axis=0)
    vcat = jnp.concatenate([vp_ref[...], vc_ref[...]], axis=0)
    row = lax.broadcasted_iota(jnp.int32, (blk, 2 * blk), 0)
    col = lax.broadcasted_iota(jnp.int32, (blk, 2 * blk), 1)
    diff = row + blk - col
    mask = (diff >= 0) & (diff <= window) & ((col >= blk) | (i > 0))
    scale = hd ** -0.5
    for g in range(n_kv):
        kg = kcat[:, g * hd:(g + 1) * hd]
        vg = vcat[:, g * hd:(g + 1) * hd]
        for j in range(group):
            h = g * group + j
            s = _qk(q_ref[:, h * hd:(h + 1) * hd], kg) * scale
            s = jnp.where(mask, s, NEG_INF)
            o = _sink_softmax_pv([(s, vg)], sink_ref[h])
            o_ref[:, h * hd:(h + 1) * hd] = o.astype(o_ref.dtype)


def _swa_prompt(q, k, v, sinks, *, nseq, seq, n_kv, hd, window):
    blk = window
    nb = seq // blk
    d_q, d_kv = q.shape[1], k.shape[1]
    group = d_q // hd // n_kv
    cur = lambda b, i: (b * nb + i, 0)
    prev = lambda b, i: (b * nb + jnp.maximum(i - 1, 0), 0)
    return pl.pallas_call(
        functools.partial(_swa_prompt_kernel, n_kv=n_kv, group=group, hd=hd, window=window),
        grid=(nseq, nb),
        in_specs=[pl.BlockSpec(memory_space=pltpu.SMEM),
                  pl.BlockSpec((blk, d_q), cur),
                  pl.BlockSpec((blk, d_kv), prev), pl.BlockSpec((blk, d_kv), cur),
                  pl.BlockSpec((blk, d_kv), prev), pl.BlockSpec((blk, d_kv), cur)],
        out_specs=pl.BlockSpec((blk, d_q), cur),
        out_shape=jax.ShapeDtypeStruct(q.shape, BF16),
        compiler_params=_cparams("arbitrary", "arbitrary"),
    )(sinks, q, k, k, v, v)


def _swa_sample_kernel(sink_ref, q_ref, kn_ref, vn_ref, kc_ref, vc_ref, o_ref, *, n_kv, group, hd):
    ds = kn_ref.shape[1]
    w = kc_ref.shape[1]
    rows = group * ds
    kc = kc_ref[0].astype(BF16)
    vc = vc_ref[0].astype(BF16)
    kn = kn_ref[0]
    vn = vn_ref[0]
    row = lax.broadcasted_iota(jnp.int32, (rows, 1), 0)
    q_head = jnp.zeros((rows, 1), jnp.int32)
    for j in range(1, group):
        q_head = jnp.where(row >= j * ds, j, q_head)
    q_pos = row - q_head * ds
    mask_c = lax.broadcasted_iota(jnp.int32, (rows, w), 1) >= q_pos
    mask_n = lax.broadcasted_iota(jnp.int32, (rows, ds), 1) <= q_pos
    scale = hd ** -0.5
    for g in range(n_kv):
        sl = slice(g * hd, (g + 1) * hd)
        sink = jnp.zeros((rows, 1), F32)
        for j in range(group):
            sink = jnp.where(q_head == j, sink_ref[g * group + j], sink)
        qg = q_ref[0, g]
        s_c = jnp.where(mask_c, _qk(qg, kc[:, sl]) * scale, NEG_INF)
        s_n = jnp.where(mask_n, _qk(qg, kn[:, sl]) * scale, NEG_INF)
        o = _sink_softmax_pv([(s_c, vc[:, sl]), (s_n, vn[:, sl])], sink)
        o_ref[0, g] = o.astype(o_ref.dtype)


def _swa_sample(q, kn, vn, kc, vc, sinks, *, n_kv, hd):
    db, ds, d_q = q.shape
    d_kv = kn.shape[2]
    w = kc.shape[1]
    group = d_q // hd // n_kv
    qg = q.reshape(db, ds, n_kv, group, hd).transpose(0, 2, 3, 1, 4).reshape(db, n_kv, group * ds, hd)
    b3 = lambda shape: pl.BlockSpec((1,) + shape, lambda b: (b, 0, 0))
    b4 = pl.BlockSpec((1, n_kv, group * ds, hd), lambda b: (b, 0, 0, 0))
    og = pl.pallas_call(
        functools.partial(_swa_sample_kernel, n_kv=n_kv, group=group, hd=hd),
        grid=(db,),
        in_specs=[pl.BlockSpec(memory_space=pltpu.SMEM), b4, b3((ds, d_kv)), b3((ds, d_kv)),
                  b3((w, d_kv)), b3((w, d_kv))],
        out_specs=b4,
        out_shape=jax.ShapeDtypeStruct(qg.shape, BF16),
        compiler_params=_cparams("arbitrary"),
    )(sinks, qg, kn, vn, kc, vc)
    return og.reshape(db, n_kv, group, ds, hd).transpose(0, 3, 1, 2, 4).reshape(db, ds, d_q)


def _bf16_round(x):
    return x.astype(BF16).astype(F32)


def _conv_tail(y, cb_ref, g_ref, b_ref):
    z = _ln_rows(y + cb_ref[...], g_ref[...], b_ref[...])
    return z * jax.nn.sigmoid(z)


def _conv_prompt_kernel(up_ref, uc_ref, w_ref, cb_ref, g_ref, b_ref, o_ref, *, conv_w, halo):
    i = pl.program_id(1)
    tq = uc_ref.shape[0]
    prev = jnp.where(i > 0, up_ref[tq - halo:, :], 0.0)
    up = _bf16_round(jnp.concatenate([prev, uc_ref[...]], axis=0))
    w = w_ref[...]
    off = halo - (conv_w - 1)
    y = None
    for j in range(conv_w):
        term = up[off + j:off + j + tq, :] * w[j:j + 1, :]
        y = term if y is None else y + term
    o_ref[...] = _conv_tail(y, cb_ref, g_ref, b_ref).astype(o_ref.dtype)


def _conv_prompt(u, conv_w, conv_b, ln_g, ln_b, *, nseq, seq):
    c = u.shape[1]
    cw = conv_w.shape[0]
    halo = -(-(cw - 1) // 8) * 8
    tq = _tile(seq, 256, BF16_SUBLANES)
    nb = seq // tq
    cur = lambda b, i: (b * nb + i, 0)
    prev = lambda b, i: (b * nb + jnp.maximum(i - 1, 0), 0)
    vec = pl.BlockSpec((1, c), lambda b, i: (0, 0))
    return pl.pallas_call(
        functools.partial(_conv_prompt_kernel, conv_w=cw, halo=halo),
        grid=(nseq, nb),
        in_specs=[pl.BlockSpec((tq, c), prev), pl.BlockSpec((tq, c), cur),
                  pl.BlockSpec((cw, c), lambda b, i: (0, 0)), vec, vec, vec],
        out_specs=pl.BlockSpec((tq, c), cur),
        out_shape=jax.ShapeDtypeStruct(u.shape, BF16),
        compiler_params=_cparams("arbitrary", "arbitrary"),
    )(u, u, conv_w, conv_b.reshape(1, c), ln_g.reshape(1, c), ln_b.reshape(1, c))


def _conv_sample_kernel(up_ref, w_ref, cb_ref, g_ref, b_ref, o_ref, *, conv_w):
    ds = o_ref.shape[1]
    up = _bf16_round(up_ref[0])
    w = w_ref[...]
    y = None
    for j in range(conv_w):
        term = up[j:j + ds, :] * w[j:j + 1, :]
        y = term if y is None else y + term
    o_ref[0] = _conv_tail(y, cb_ref, g_ref, b_ref).astype(o_ref.dtype)


def _conv_sample(up, conv_w, conv_b, ln_g, ln_b, *, ds):
    db, length, c = up.shape
    cw = conv_w.shape[0]
    vec = pl.BlockSpec((1, c), lambda b: (0, 0))
    return pl.pallas_call(
        functools.partial(_conv_sample_kernel, conv_w=cw),
        grid=(db,),
        in_specs=[pl.BlockSpec((1, length, c), lambda b: (b, 0, 0)), pl.BlockSpec((cw, c), lambda b: (0, 0)),
                  vec, vec, vec],
        out_specs=pl.BlockSpec((1, ds, c), lambda b: (b, 0, 0)),
        out_shape=jax.ShapeDtypeStruct((db, ds, c), BF16),
        compiler_params=_cparams("arbitrary"),
    )(up, conv_w, conv_b.reshape(1, c), ln_g.reshape(1, c), ln_b.reshape(1, c))


def _wkv_kernel(r_ref, d_ref, k_ref, v_ref, q_ref, p_ref, s0_ref, o_ref, st_ref, m_scr, *, round_state):
    c = pl.program_id(1)
    tc, hc, _ = r_ref.shape
    rnd = _bf16_round if round_state else (lambda z: z)

    @pl.when(c == 0)
    def _():
        m_scr[...] = s0_ref[...]

    def row(ref, t, k):
        return ref[t, k:k + 1, :]

    u0 = None
    for k in range(hc):
        term = rnd(m_scr[k]) * rnd(row(q_ref, 0, k))
        u0 = term if u0 is None else u0 + term

    def step(t, u):
        t_next = jnp.minimum(t + 1, tc - 1)
        v_t = v_ref[t]
        o = u_next = None
        for k in range(hc):
            m_k = m_scr[k] * row(d_ref, t, k) - u * row(p_ref, t, k) + v_t * row(k_ref, t, k)
            m_scr[k] = m_k
            m_b = rnd(m_k)
            t_o = m_b * rnd(row(r_ref, t, k))
            t_u = m_b * rnd(row(q_ref, t_next, k))
            o = t_o if o is None else o + t_o
            u_next = t_u if u_next is None else u_next + t_u
        o_ref[t] = o
        return u_next

    lax.fori_loop(0, tc, step, u0)

    @pl.when(c == pl.num_programs(1) - 1)
    def _():
        st_ref[...] = m_scr[...]


def _wkv(r, d, k, v, q, p, s0, *, hc, round_state):
    nseq, t, dm = r.shape
    n_heads = dm // hc
    n_prob = nseq * n_heads
    lanes = _tile(n_prob, LANES, 1)
    tc = _tile(t, 64, 1)
    to_lanes = lambda z: z.reshape(nseq, t, n_heads, hc).transpose(1, 3, 0, 2).reshape(t, hc, n_prob)
    seq_spec = pl.BlockSpec((tc, hc, lanes), lambda g, c: (c, 0, g))
    st_spec = pl.BlockSpec((hc, hc, lanes), lambda g, c: (0, 0, g))
    o, st = pl.pallas_call(
        functools.partial(_wkv_kernel, round_state=round_state),
        grid=(n_prob // lanes, t // tc),
        in_specs=[seq_spec] * 6 + [st_spec],
        out_specs=[seq_spec, st_spec],
        out_shape=[jax.ShapeDtypeStruct((t, hc, n_prob), F32), jax.ShapeDtypeStruct((hc, hc, n_prob), F32)],
        scratch_shapes=[pltpu.VMEM((hc, hc, lanes), F32)],
        compiler_params=_cparams("arbitrary", "arbitrary"),
    )(*[to_lanes(z) for z in (r, d, k, v, q, p)], s0.transpose(3, 2, 0, 1).reshape(hc, hc, n_prob))
    o = o.reshape(t, hc, nseq, n_heads).transpose(2, 0, 3, 1).reshape(nseq, t, dm)
    st = st.reshape(hc, hc, nseq, n_heads).transpose(2, 3, 1, 0)
    return o, st


def _router_kernel(x_ref, w_ref, idx_ref, gate_ref, *, n_exp):
    logits = jnp.dot(x_ref[...].astype(BF16), w_ref[...].astype(BF16), preferred_element_type=F32)
    lane_i = lax.broadcasted_iota(jnp.int32, logits.shape, 1)
    lane = lane_i.astype(F32)
    logits = jnp.where(lane_i < n_exp, logits, NEG_INF)
    m1 = jnp.max(logits, axis=-1, keepdims=True)
    i1 = jnp.min(jnp.where(logits == m1, lane, float(LANES)), axis=-1, keepdims=True)
    rest = jnp.where(lane == i1, NEG_INF, logits)
    m2 = jnp.max(rest, axis=-1, keepdims=True)
    i2 = jnp.min(jnp.where(rest == m2, lane, float(LANES)), axis=-1, keepdims=True)
    e2 = jnp.exp(m2 - m1)
    den = 1.0 + e2
    idx_ref[...] = jnp.where(lane_i == 0, i1, i2).astype(jnp.int32)
    gate_ref[...] = jnp.where(lane == 0, 1.0 / den, e2 / den)


def _router(x, w_router):
    m, d = x.shape
    n_exp = w_router.shape[1]
    w = jnp.pad(w_router, ((0, 0), (0, LANES - n_exp)))
    tm = _tile(m, 512, 8)
    out = pl.BlockSpec((tm, LANES), lambda i: (i, 0))
    idx, gate = pl.pallas_call(
        functools.partial(_router_kernel, n_exp=n_exp),
        grid=(m // tm,),
        in_specs=[pl.BlockSpec((tm, d), lambda i: (i, 0)), pl.BlockSpec((d, LANES), lambda i: (0, 0))],
        out_specs=[out, out],
        out_shape=[jax.ShapeDtypeStruct((m, LANES), jnp.int32), jax.ShapeDtypeStruct((m, LANES), F32)],
        compiler_params=_cparams("arbitrary"),
    )(x, w)
    return idx[:, :TOP_K], gate[:, :TOP_K]


def _gather_kernel(src_ref, x_hbm, o_ref, buf, sem):
    tr = o_ref.shape[0]
    base = pl.program_id(0) * tr

    def row_copy(i):
        return pltpu.make_async_copy(x_hbm.at[pl.ds(src_ref[base + i], 1)], buf.at[pl.ds(i, 1)], sem)

    def start(i, c):
        row_copy(i).start()
        return c

    def wait(i, c):
        row_copy(i).wait()
        return c

    lax.fori_loop(0, tr, start, 0, unroll=8)
    lax.fori_loop(0, tr, wait, 0, unroll=8)
    o_ref[...] = buf[...].astype(o_ref.dtype)


def _gather_rows(x, src, *, tr):
    rows = src.shape[0]
    d = x.shape[1]
    return pl.pallas_call(
        _gather_kernel,
        grid_spec=pltpu.PrefetchScalarGridSpec(
            num_scalar_prefetch=1,
            grid=(rows // tr,),
            in_specs=[pl.BlockSpec(memory_space=pl.ANY)],
            out_specs=pl.BlockSpec((tr, d), lambda i, src: (i, 0)),
            scratch_shapes=[pltpu.VMEM((tr, d), F32), pltpu.SemaphoreType.DMA(())],
        ),
        out_shape=jax.ShapeDtypeStruct((rows, d), BF16),
        compiler_params=_cparams("arbitrary"),
    )(src, x)


def _combine_ln_kernel(pos_ref, y_hbm, x_ref, gate_ref, g_ref, b_ref, o_ref, buf, sem, *, alpha):
    tr = o_ref.shape[0]
    base = pl.program_id(0) * tr

    def row_copy(i, slot):
        src = pos_ref[(base + i) * TOP_K + slot]
        return pltpu.make_async_copy(y_hbm.at[pl.ds(src, 1)], buf.at[slot, pl.ds(i, 1)], sem)

    def start(i, c):
        for slot in range(TOP_K):
            row_copy(i, slot).start()
        return c

    def wait(i, c):
        for slot in range(TOP_K):
            row_copy(i, slot).wait()
        return c

    lax.fori_loop(0, tr, start, 0, unroll=8)
    lax.fori_loop(0, tr, wait, 0, unroll=8)
    gates = gate_ref[...]
    f = gates[:, 0:1] * buf[0] + gates[:, 1:2] * buf[1]
    o_ref[...] = _ln_rows(alpha * x_ref[...] + f, g_ref[...], b_ref[...])


def _combine_ln(y_sorted, pos, gates, x, g, b, alpha):
    m, d = x.shape
    tr = _tile(m, 128, 8)
    row = pl.BlockSpec((tr, d), lambda i, pos: (i, 0))
    vec = pl.BlockSpec((1, d), lambda i, pos: (0, 0))
    return pl.pallas_call(
        functools.partial(_combine_ln_kernel, alpha=alpha),
        grid_spec=pltpu.PrefetchScalarGridSpec(
            num_scalar_prefetch=1,
            grid=(m // tr,),
            in_specs=[pl.BlockSpec(memory_space=pl.ANY), row,
                      pl.BlockSpec((tr, TOP_K), lambda i, pos: (i, 0)), vec, vec],
            out_specs=row,
            scratch_shapes=[pltpu.VMEM((TOP_K, tr, d), F32), pltpu.SemaphoreType.DMA(())],
        ),
        out_shape=jax.ShapeDtypeStruct((m, d), F32),
        compiler_params=_cparams("arbitrary"),
    )(pos, y_sorted, x, gates, g.reshape(1, d), b.reshape(1, d))


def _route_tables(top_i, n_exp, tm):
    m = top_i.shape[0]
    n_rows = m * TOP_K
    n_tiles = -(-(n_rows + n_exp * (tm - 1)) // tm)
    e_flat = top_i.reshape(-1)
    order = jnp.argsort(e_flat, stable=True)
    counts = jnp.sum(e_flat[:, None] == jnp.arange(n_exp, dtype=jnp.int32)[None, :], axis=0).astype(jnp.int32)
    padded = (counts + tm - 1) // tm * tm
    pad_end = jnp.cumsum(padded)
    pad_start = pad_end - padded
    cnt_start = jnp.cumsum(counts) - counts
    e_sorted = e_flat[order]
    dest_sorted = pad_start[e_sorted] + jnp.arange(n_rows, dtype=jnp.int32) - cnt_start[e_sorted]
    pos = jnp.zeros((n_rows,), jnp.int32).at[order].set(dest_sorted)
    src = jnp.zeros((n_tiles * tm,), jnp.int32).at[dest_sorted].set((order // TOP_K).astype(jnp.int32))
    n_valid = pad_end[-1] // tm
    t_idx = jnp.arange(n_tiles, dtype=jnp.int32)
    t_eff = jnp.minimum(t_idx, jnp.maximum(n_valid - 1, 0))
    te = jnp.searchsorted(pad_end, t_eff * tm, side="right").astype(jnp.int32)
    te = jnp.minimum(te, n_exp - 1)
    tf = jnp.concatenate([jnp.ones((1,), jnp.int32), (te[1:] != te[:-1]).astype(jnp.int32)])
    tv = (t_idx < n_valid).astype(jnp.int32)
    return src, pos, (te, tf, tv, t_eff, t_idx)


def _rope(x, pos, hd):
    half = hd // 2
    inv = jnp.power(ROPE_THETA, -jnp.arange(half, dtype=F32) / half)
    ang = pos.astype(F32)[:, None] * inv[None, :]
    cos, sin = jnp.cos(ang)[:, None, :], jnp.sin(ang)[:, None, :]
    xh = x.reshape(x.shape[0], -1, hd)
    x1, x2 = xh[..., :half], xh[..., half:]
    return jnp.concatenate([x1 * cos - x2 * sin, x2 * cos + x1 * sin], axis=-1).reshape(x.shape)


def _even_mixer(x_bf, dims, cache_k, cache_v, state_conv, w_in, b_in, sinks, conv_w, conv_b, conv_g, conv_bn,
                w_out, b_out):
    nseq, seq, db, ds, n_p = dims
    m = x_bf.shape[0]
    _, w, n_kv, hd = cache_k.shape
    d_kv = n_kv * hd
    d_q = sinks.shape[0] * hd
    d_conv = state_conv.shape[-1]
    cw = conv_w.shape[0]
    tm = _tile(m, 640, BF16_SUBLANES)
    col_a, col_g = d_q + 2 * d_kv, d_q + 2 * d_kv + d_conv

    def proj(col, width):
        tn = _tile(math.gcd(col, width), 512, LANES)
        return _dense(x_bf, w_in, tm=tm, tn=tn, bias=b_in, col=col, n_out=width)

    pos = jnp.concatenate([jnp.tile(jnp.arange(seq, dtype=jnp.int32), nseq),
                           jnp.tile(PAST_LEN + jnp.arange(ds, dtype=jnp.int32), db)])
    q = _rope(proj(0, d_q), pos, hd).astype(BF16)
    if d_kv % LANES == 0:
        k, v = proj(d_q, d_kv), proj(d_q + d_kv, d_kv)
    else:
        kv = proj(d_q, col_a - d_q)
        k, v = kv[:, :d_kv], kv[:, d_kv:]
    k = _rope(k, pos, hd)
    u = _mm(x_bf, [w_in[None], w_in[None]], tm=tm, tn=_tile(math.gcd(col_a, col_g, d_conv), 256, LANES),
            cols=(col_a, col_g), n_out=d_conv, biases=[b_in, b_in], act="glu")
    k_bf, v_bf = k.astype(BF16), v.astype(BF16)

    att_p = _swa_prompt(q[:n_p], k_bf[:n_p], v_bf[:n_p], sinks, nseq=nseq, seq=seq, n_kv=n_kv, hd=hd, window=w)
    cache_k2 = cache_k.reshape(db, w, d_kv)
    cache_v2 = cache_v.reshape(db, w, d_kv)
    att_s = _swa_sample(q[n_p:].reshape(db, ds, d_q), k_bf[n_p:].reshape(db, ds, d_kv),
                        v_bf[n_p:].reshape(db, ds, d_kv), cache_k2, cache_v2, sinks, n_kv=n_kv, hd=hd)
    conv_p = _conv_prompt(u[:n_p], conv_w, conv_b, conv_g, conv_bn, nseq=nseq, seq=seq)
    up_s = jnp.concatenate([state_conv, u[n_p:].reshape(db, ds, d_conv)], axis=1)
    conv_s = _conv_sample(up_s, conv_w, conv_b, conv_g, conv_bn, ds=ds)

    mix = jnp.concatenate([jnp.concatenate([att_p, att_s.reshape(db * ds, d_q)], axis=0),
                           jnp.concatenate([conv_p, conv_s.reshape(db * ds, d_conv)], axis=0)], axis=1)
    y = _dense(mix, w_out, tm=tm, tn=_tile(w_out.shape[1], 512, LANES), bias=b_out)

    kp = k[:n_p].reshape(nseq, seq, n_kv, hd)[:, -w:]
    vp = v[:n_p].reshape(nseq, seq, n_kv, hd)[:, -w:]
    ks = jnp.concatenate([cache_k, k[n_p:].reshape(db, ds, n_kv, hd)], axis=1)[:, -w:]
    vs = jnp.concatenate([cache_v, v[n_p:].reshape(db, ds, n_kv, hd)], axis=1)[:, -w:]
    cp = u[:n_p].reshape(nseq, seq, d_conv)[:, -(cw - 1):]
    cs = up_s[:, -(cw - 1):]
    return y, (kp, vp, ks, vs, cp, cs)


def _rwkv_mixer(x, dims, state_shift, state_wkv, mu, w0, w1, w2, a0, a1, a2, g1, g2, k_k, k_a, r_k, w_r, w_k, w_v,
                w_o, lnx_g, lnx_b):
    nseq, seq, db, ds, n_p = dims
    m, d = x.shape
    n_heads, hc = state_wkv.shape[1], state_wkv.shape[-1]
    xp = x[:n_p].reshape(nseq, seq, d)
    xs = x[n_p:].reshape(db, ds, d)
    prev_p = jnp.concatenate([jnp.zeros((nseq, 1, d), F32), xp[:, :-1]], axis=1)
    prev_s = jnp.concatenate([state_shift[:, None], xs[:, :-1]], axis=1)

    def big(z, wt):
        return _dense(z, wt, tm=_tile(z.shape[0], 640, BF16_SUBLANES), tn=_tile(wt.shape[1], 512, LANES))

    def part(x_, prev_, n_s, t, state, round_state):
        n = n_s * t
        x_, prev_ = x_.reshape(n, d), prev_.reshape(n, d)
        xx = prev_ - x_
        xr, xw, xk, xv, xa, xg = [(x_ + xx * mu[i]).astype(BF16) for i in range(6)]
        r_ = big(xr, w_r)
        k_ = big(xk, w_k)
        v_ = big(xv, w_v)
        lw_ = big(jnp.tanh(big(xw, w1)).astype(BF16), w2)
        la_ = big(big(xa, a1).astype(BF16), a2)
        g_ = big(jax.nn.sigmoid(big(xg, g1)).astype(BF16), g2)
        heads = lambda z: z.reshape(n, n_heads, hc)
        w = -jax.nn.softplus(-(w0 + lw_)) - 0.5
        a = jax.nn.sigmoid(a0 + la_)
        kk = heads(k_ * k_k)
        kk = (kk * lax.rsqrt(jnp.maximum(jnp.sum(kk * kk, axis=-1, keepdims=True), 1e-24))).reshape(n, d)
        k_ = k_ * (1 + (a - 1) * k_a)
        decay = jnp.exp(-jnp.exp(w))
        wkv_in = [z.reshape(n_s, t, d) for z in (r_, decay, k_, v_, kk, kk * a)]
        o, st = _wkv(*wkv_in, state, hc=hc, round_state=round_state)
        o = heads(o.reshape(n, d))
        o_mean = jnp.mean(o, axis=-1, keepdims=True)
        o_var = jnp.mean(jnp.square(o - o_mean), axis=-1, keepdims=True)
        o = ((o - o_mean) * lax.rsqrt(o_var + GN_EPS)).reshape(n, d) * lnx_g + lnx_b
        bonus = jnp.sum(heads(r_) * heads(k_) * r_k, axis=-1, keepdims=True) * heads(v_)
        return ((o + bonus.reshape(n, d)) * g_).astype(BF16), st

    og_p, wkv_p = part(xp, prev_p, nseq, seq, jnp.zeros((nseq, n_heads, hc, hc), F32), False)
    og_s, wkv_s = part(xs, prev_s, db, ds, state_wkv, True)
    y = big(jnp.concatenate([og_p, og_s], axis=0), w_o)
    return y, (xp[:, -1], xs[:, -1], wkv_p, wkv_s)


def _swiglu_dense(x_bf, wg, wu, wd, *, tm):
    d_ff = wg.shape[1]
    h = _mm(x_bf, [wg[None], wu[None]], tm=tm, tn=_tile(d_ff, 256, LANES), act="swiglu", out_dtype=BF16)
    nk = 2 if d_ff % (2 * LANES) == 0 and d_ff > 8192 else 1
    return _dense(h, wd, tm=tm, tn=_tile(wd.shape[1], 512, LANES), nk=nk)


def _moe(x, x_ln_args, w_router, wg, wu, wd, alpha):
    m, d = x.shape
    n_exp, _, d_ff = wg.shape
    tm = 512 if m * TOP_K >= 4096 else 64
    top_i, gates = _router(x, w_router)
    src, pos, tiles = _route_tables(top_i, n_exp, tm)
    xs = _gather_rows(x, src, tr=_tile(src.shape[0], 256, BF16_SUBLANES))
    h = _mm(xs, [wg, wu], tm=tm, tn=_tile(d_ff, 512, LANES), act="swiglu", out_dtype=BF16, tiles=tiles)
    nk = 4 if d_ff % (4 * LANES) == 0 and d_ff > 8192 else 1
    y = None
    for kk in range(nk):
        y = _mm(h, [wd], tm=tm, tn=_tile(d, 512, LANES), nk=nk, kk=kk, add=y, tiles=tiles)
    g, b = x_ln_args
    return _combine_ln(y, pos, gates, x, g, b, alpha)


def kernel(x_prompt, x_sample, cache_swa_k, cache_swa_v, state_conv, state_shift, state_wkv, ln_g, ln_b, w_in, b_in,
           sinks, conv_w, conv_b, conv_ln_g, conv_ln_b, w_out, b_out, ffn_w_gate, ffn_w_up, ffn_w_down, mu, w0, w1,
           w2, a0, a1, a2, g1, g2, k_k, k_a, r_k, w_r, w_k, w_v, w_o, lnx_g, lnx_b, w_router, moe_w_gate, moe_w_up,
           moe_w_down):
    nseq, seq, d = x_prompt.shape
    db, ds, _ = x_sample.shape
    n_p = nseq * seq
    dims = (nseq, seq, db, ds, n_p)
    depth = ln_g.shape[0]
    alpha = (2 * depth) ** 0.25
    x = jnp.concatenate([x_prompt.reshape(n_p, d), x_sample.reshape(db * ds, d)], axis=0)
    m = x.shape[0]
    tm = _tile(m, 640, BF16_SUBLANES)
    x_bf = x.astype(BF16)
    even_out, odd_out = [], []
    for layer in range(depth):
        i = layer // 2
        if layer % 2 == 0:
            y, states = _even_mixer(x_bf, dims, cache_swa_k[i], cache_swa_v[i], state_conv[i], w_in[i], b_in[i],
                                    sinks[i], conv_w[i], conv_b[i], conv_ln_g[i], conv_ln_b[i], w_out[i], b_out[i])
            even_out.append(states)
        else:
            y, states = _rwkv_mixer(x, dims, state_shift[i], state_wkv[i], mu[i], w0[i], w1[i], w2[i], a0[i], a1[i],
                                    a2[i], g1[i], g2[i], k_k[i], k_a[i], r_k[i], w_r[i], w_k[i], w_v[i], w_o[i],
                                    lnx_g[i], lnx_b[i])
            odd_out.append(states)
        x, x_bf = _add_ln(x, y, ln_g[layer, 0], ln_b[layer, 0], alpha)
        if layer % 2 == 0:
            f = _swiglu_dense(x_bf, ffn_w_gate[i], ffn_w_up[i], ffn_w_down[i], tm=tm)
            x, x_bf = _add_ln(x, f, ln_g[layer, 1], ln_b[layer, 1], alpha)
        else:
            x = _moe(x, (ln_g[layer, 1], ln_b[layer, 1]), w_router[i], moe_w_gate[i], moe_w_up[i], moe_w_down[i],
                     alpha)
            x_bf = x.astype(BF16)
    stack = lambda outs, j: jnp.stack([o[j] for o in outs])
    return (x[:n_p].reshape(nseq, seq, d), x[n_p:].reshape(db, ds, d),
            stack(even_out, 0), stack(even_out, 1), stack(even_out, 2), stack(even_out, 3),
            stack(even_out, 4), stack(even_out, 5),
            stack(odd_out, 0), stack(odd_out, 1), stack(odd_out, 2), stack(odd_out, 3))
```

```python
import functools
import math

import jax
import jax.numpy as jnp
from jax import lax
from jax.experimental import pallas as pl
from jax.experimental.pallas import tpu as pltpu

F32 = jnp.float32
BF16 = jnp.bfloat16

PAST_LEN = 16384
ROPE_THETA = 10000.0
LN_EPS = 1e-5
GN_EPS = 64e-5
TOP_K = 2
NEG_INF = float("-inf")

VMEM_LIMIT_BYTES = 58 * 1024 * 1024
LANES = 128
BF16_SUBLANES = 16


def _cparams(*sem):
    return pltpu.CompilerParams(dimension_semantics=sem, vmem_limit_bytes=VMEM_LIMIT_BYTES)


def _tile(n, target, mult):
    best = None
    for t in range(mult, min(n, target) + 1, mult):
        if n % t == 0:
            best = t
    return best if best is not None else n


def _mm_kernel(te_ref, tf_ref, tv_ref, tx_ref, to_ref, *refs, n_w, n_bias, has_add, act):
    del te_ref, tx_ref, to_ref
    x_ref = refs[0]
    w_refs = refs[1:1 + n_w]
    b_refs = refs[1 + n_w:1 + n_w + n_bias]
    pos = 1 + n_w + n_bias
    a_ref = None
    if has_add:
        a_ref = refs[pos]
        pos += 1
    o_ref = refs[pos]
    wb_refs = refs[pos + 1:pos + 1 + n_w]
    t = pl.program_id(1)

    @pl.when(tf_ref[t] == 1)
    def _():
        for w_ref, wb_ref in zip(w_refs, wb_refs):
            wb_ref[...] = w_ref[0].astype(BF16)

    @pl.when(tv_ref[t] == 1)
    def _():
        x = x_ref[...]
        accs = [jnp.dot(x, wb_ref[...], preferred_element_type=F32) for wb_ref in wb_refs]
        if n_bias:
            accs = [acc + b_ref[...] for acc, b_ref in zip(accs, b_refs)]
        if act == "swiglu":
            acc = accs[0] * jax.nn.sigmoid(accs[0]) * accs[1]
        elif act == "glu":
            acc = accs[0] * jax.nn.sigmoid(accs[1])
        else:
            acc = accs[0]
        if has_add:
            acc = acc + a_ref[...]
        o_ref[...] = acc.astype(o_ref.dtype)

    @pl.when(tv_ref[t] == 0)
    def _():
        o_ref[...] = jnp.zeros(o_ref.shape, o_ref.dtype)


def _mm(x, ws, *, tm, tn, nk=1, kk=0, cols=None, n_out=None, biases=None, add=None, act=None, out_dtype=F32,
        tiles=None):
    m, k = x.shape
    n = ws[0].shape[-1]
    n_out = n if n_out is None else n_out
    cols = (0,) * len(ws) if cols is None else cols
    assert m % tm == 0 and n_out % tn == 0 and k % nk == 0 and all(c % tn == 0 for c in cols)
    tk = k // nk
    nt = m // tm
    if tiles is None:
        idx = jnp.arange(nt, dtype=jnp.int32)
        tiles = (jnp.zeros((nt,), jnp.int32), (idx == 0).astype(jnp.int32), jnp.ones((nt,), jnp.int32), idx, idx)
    in_specs = [pl.BlockSpec((tm, tk), lambda j, t, te, tf, tv, tx, to: (tx[t], kk))]
    args = [x]
    for w, c in zip(ws, cols):
        in_specs.append(pl.BlockSpec((1, tk, tn), lambda j, t, te, tf, tv, tx, to, c=c: (te[t], kk, c // tn + j)))
        args.append(w)
    for b, c in zip(biases or (), cols):
        in_specs.append(pl.BlockSpec((1, tn), lambda j, t, te, tf, tv, tx, to, c=c: (0, c // tn + j)))
        args.append(b.reshape(1, n))
    out_spec = pl.BlockSpec((tm, tn), lambda j, t, te, tf, tv, tx, to: (to[t], j))
    if add is not None:
        in_specs.append(out_spec)
        args.append(add)
    kern = functools.partial(_mm_kernel, n_w=len(ws), n_bias=len(biases or ()), has_add=add is not None, act=act)
    return pl.pallas_call(
        kern,
        grid_spec=pltpu.PrefetchScalarGridSpec(
            num_scalar_prefetch=5,
            grid=(n_out // tn, nt),
            in_specs=in_specs,
            out_specs=out_spec,
            scratch_shapes=[pltpu.VMEM((tk, tn), BF16) for _ in ws],
        ),
        out_shape=jax.ShapeDtypeStruct((m, n_out), out_dtype),
        compiler_params=_cparams("arbitrary", "arbitrary"),
    )(*tiles, *args)


def _dense(x, w, *, tm, tn, nk=1, bias=None, col=0, n_out=None, out_dtype=F32):
    out = None
    for kk in range(nk):
        last = kk == nk - 1
        out = _mm(x, [w[None]], tm=tm, tn=tn, nk=nk, kk=kk, cols=(col,), n_out=n_out,
                  biases=[bias] if bias is not None and kk == 0 else None, add=out,
                  out_dtype=out_dtype if last else F32)
    return out


def _ln_rows(z, g, b):
    mean = jnp.mean(z, axis=-1, keepdims=True)
    zc = z - mean
    var = jnp.mean(zc * zc, axis=-1, keepdims=True)
    return zc * lax.rsqrt(var + LN_EPS) * g + b


def _add_ln_kernel(x_ref, y_ref, g_ref, b_ref, o_ref, ob_ref, *, alpha):
    out = _ln_rows(alpha * x_ref[...] + y_ref[...], g_ref[...], b_ref[...])
    o_ref[...] = out
    ob_ref[...] = out.astype(BF16)


def _add_ln(x, y, g, b, alpha):
    m, d = x.shape
    tm = _tile(m, 256, BF16_SUBLANES)
    row = pl.BlockSpec((tm, d), lambda i: (i, 0))
    vec = pl.BlockSpec((1, d), lambda i: (0, 0))
    return pl.pallas_call(
        functools.partial(_add_ln_kernel, alpha=alpha),
        grid=(m // tm,),
        in_specs=[row, row, vec, vec],
        out_specs=[row, row],
        out_shape=[jax.ShapeDtypeStruct((m, d), F32), jax.ShapeDtypeStruct((m, d), BF16)],
        compiler_params=_cparams("arbitrary"),
    )(x, y, g.reshape(1, d), b.reshape(1, d))


def _sink_softmax_pv(pieces, sink):
    m = sink
    for s, _ in pieces:
        m = jnp.maximum(jnp.max(s, axis=-1, keepdims=True), m)
    ps = [jnp.exp(s - m) for s, _ in pieces]
    den = jnp.exp(sink - m)
    for p in ps:
        den = den + jnp.sum(p, axis=-1, keepdims=True)
    out = None
    for p, (_, v) in zip(ps, pieces):
        o = jnp.dot((p / den).astype(BF16), v, preferred_element_type=F32)
        out = o if out is None else out + o
    return out


def _qk(q, k):
    return lax.dot_general(q, k, (((1,), (1,)), ((), ())), preferred_element_type=F32)


def _swa_prompt_kernel(sink_ref, q_ref, kp_ref, kc_ref, vp_ref, vc_ref, o_ref, *, n_kv, group, hd, window):
    i = pl.program_id(1)
    blk = q_ref.shape[0]
    kcat = jnp.concatenate([kp_ref[...], kc_ref[...]], axis=0)
    vcat = jnp.concatenate([vp_ref[...], vc_ref[...]], axis=0)
    row = lax.broadcasted_iota(jnp.int32, (blk, 2 * blk), 0)
    col = lax.broadcasted_iota(jnp.int32, (blk, 2 * blk), 1)
    diff = row + blk - col
    mask = (diff >= 0) & (diff <= window) & ((col >= blk) | (i > 0))
    scale = hd ** -0.5
    for g in range(n_kv):
        kg = kcat[:, g * hd:(g + 1) * hd]
        vg = vcat[:, g * hd:(g + 1) * hd]
        for j in range(group):
            h = g * group + j
            s = _qk(q_ref[:, h * hd:(h + 1) * hd], kg) * scale
            s = jnp.where(mask, s, NEG_INF)
            o = _sink_softmax_pv([(s, vg)], sink_ref[h])
            o_ref[:, h * hd:(h + 1) * hd] = o.astype(o_ref.dtype)


def _swa_prompt(q, k, v, sinks, *, nseq, seq, n_kv, hd, window):
    blk = window
    nb = seq // blk
    d_q, d_kv = q.shape[1], k.shape[1]
    group = d_q // hd // n_kv
    cur = lambda b, i: (b * nb + i, 0)
    prev = lambda b, i: (b * nb + jnp.maximum(i - 1, 0), 0)
    return pl.pallas_call(
        functools.partial(_swa_prompt_kernel, n_kv=n_kv, group=group, hd=hd, window=window),
        grid=(nseq, nb),
        in_specs=[pl.BlockSpec(memory_space=pltpu.SMEM),
                  pl.BlockSpec((blk, d_q), cur),
                  pl.BlockSpec((blk, d_kv), prev), pl.BlockSpec((blk, d_kv), cur),
                  pl.BlockSpec((blk, d_kv), prev), pl.BlockSpec((blk, d_kv), cur)],
        out_specs=pl.BlockSpec((blk, d_q), cur),
        out_shape=jax.ShapeDtypeStruct(q.shape, BF16),
        compiler_params=_cparams("arbitrary", "arbitrary"),
    )(sinks, q, k, k, v, v)


def _swa_sample_kernel(sink_ref, q_ref, kn_ref, vn_ref, kc_ref, vc_ref, o_ref, *, n_kv, group, hd):
    ds = kn_ref.shape[1]
    w = kc_ref.shape[1]
    rows = group * ds
    kc = kc_ref[0].astype(BF16)
    vc = vc_ref[0].astype(BF16)
    kn = kn_ref[0]
    vn = vn_ref[0]
    row = lax.broadcasted_iota(jnp.int32, (rows, 1), 0)
    q_head = jnp.zeros((rows, 1), jnp.int32)
    for j in range(1, group):
        q_head = jnp.where(row >= j * ds, j, q_head)
    q_pos = row - q_head * ds
    mask_c = lax.broadcasted_iota(jnp.int32, (rows, w), 1) >= q_pos
    mask_n = lax.broadcasted_iota(jnp.int32, (rows, ds), 1) <= q_pos
    scale = hd ** -0.5
    for g in range(n_kv):
        sl = slice(g * hd, (g + 1) * hd)
        sink = jnp.zeros((rows, 1), F32)
        for j in range(group):
            sink = jnp.where(q_head == j, sink_ref[g * group + j], sink)
        qg = q_ref[0, g]
        s_c = jnp.where(mask_c, _qk(qg, kc[:, sl]) * scale, NEG_INF)
        s_n = jnp.where(mask_n, _qk(qg, kn[:, sl]) * scale, NEG_INF)
        o = _sink_softmax_pv([(s_c, vc[:, sl]), (s_n, vn[:, sl])], sink)
        o_ref[0, g] = o.astype(o_ref.dtype)


def _swa_sample(q, kn, vn, kc, vc, sinks, *, n_kv, hd):
    db, ds, d_q = q.shape
    d_kv = kn.shape[2]
    w = kc.shape[1]
    group = d_q // hd // n_kv
    qg = q.reshape(db, ds, n_kv, group, hd).transpose(0, 2, 3, 1, 4).reshape(db, n_kv, group * ds, hd)
    b3 = lambda shape: pl.BlockSpec((1,) + shape, lambda b: (b, 0, 0))
    b4 = pl.BlockSpec((1, n_kv, group * ds, hd), lambda b: (b, 0, 0, 0))
    og = pl.pallas_call(
        functools.partial(_swa_sample_kernel, n_kv=n_kv, group=group, hd=hd),
        grid=(db,),
        in_specs=[pl.BlockSpec(memory_space=pltpu.SMEM), b4, b3((ds, d_kv)), b3((ds, d_kv)),
                  b3((w, d_kv)), b3((w, d_kv))],
        out_specs=b4,
        out_shape=jax.ShapeDtypeStruct(qg.shape, BF16),
        compiler_params=_cparams("arbitrary"),
    )(sinks, qg, kn, vn, kc, vc)
    return og.reshape(db, n_kv, group, ds, hd).transpose(0, 3, 1, 2, 4).reshape(db, ds, d_q)


def _bf16_round(x):
    return x.astype(BF16).astype(F32)


def _conv_tail(y, cb_ref, g_ref, b_ref):
    z = _ln_rows(y + cb_ref[...], g_ref[...], b_ref[...])
    return z * jax.nn.sigmoid(z)


def _conv_prompt_kernel(up_ref, uc_ref, w_ref, cb_ref, g_ref, b_ref, o_ref, *, conv_w, halo):
    i = pl.program_id(1)
    tq = uc_ref.shape[0]
    prev = jnp.where(i > 0, up_ref[tq - halo:, :], 0.0)
    up = _bf16_round(jnp.concatenate([prev, uc_ref[...]], axis=0))
    w = w_ref[...]
    off = halo - (conv_w - 1)
    y = None
    for j in range(conv_w):
        term = up[off + j:off + j + tq, :] * w[j:j + 1, :]
        y = term if y is None else y + term
    o_ref[...] = _conv_tail(y, cb_ref, g_ref, b_ref).astype(o_ref.dtype)


def _conv_prompt(u, conv_w, conv_b, ln_g, ln_b, *, nseq, seq):
    c = u.shape[1]
    cw = conv_w.shape[0]
    halo = -(-(cw - 1) // 8) * 8
    tq = _tile(seq, 256, BF16_SUBLANES)
    nb = seq // tq
    cur = lambda b, i: (b * nb + i, 0)
    prev = lambda b, i: (b * nb + jnp.maximum(i - 1, 0), 0)
    vec = pl.BlockSpec((1, c), lambda b, i: (0, 0))
    return pl.pallas_call(
        functools.partial(_conv_prompt_kernel, conv_w=cw, halo=halo),
        grid=(nseq, nb),
        in_specs=[pl.BlockSpec((tq, c), prev), pl.BlockSpec((tq, c), cur),
                  pl.BlockSpec((cw, c), lambda b, i: (0, 0)), vec, vec, vec],
        out_specs=pl.BlockSpec((tq, c), cur),
        out_shape=jax.ShapeDtypeStruct(u.shape, BF16),
        compiler_params=_cparams("arbitrary", "arbitrary"),
    )(u, u, conv_w, conv_b.reshape(1, c), ln_g.reshape(1, c), ln_b.reshape(1, c))


def _conv_sample_kernel(up_ref, w_ref, cb_ref, g_ref, b_ref, o_ref, *, conv_w):
    ds = o_ref.shape[1]
    up = _bf16_round(up_ref[0])
    w = w_ref[...]
    y = None
    for j in range(conv_w):
        term = up[j:j + ds, :] * w[j:j + 1, :]
        y = term if y is None else y + term
    o_ref[0] = _conv_tail(y, cb_ref, g_ref, b_ref).astype(o_ref.dtype)


def _conv_sample(up, conv_w, conv_b, ln_g, ln_b, *, ds):
    db, length, c = up.shape
    cw = conv_w.shape[0]
    vec = pl.BlockSpec((1, c), lambda b: (0, 0))
    return pl.pallas_call(
        functools.partial(_conv_sample_kernel, conv_w=cw),
        grid=(db,),
        in_specs=[pl.BlockSpec((1, length, c), lambda b: (b, 0, 0)), pl.BlockSpec((cw, c), lambda b: (0, 0)),
                  vec, vec, vec],
        out_specs=pl.BlockSpec((1, ds, c), lambda b: (b, 0, 0)),
        out_shape=jax.ShapeDtypeStruct((db, ds, c), BF16),
        compiler_params=_cparams("arbitrary"),
    )(up, conv_w, conv_b.reshape(1, c), ln_g.reshape(1, c), ln_b.reshape(1, c))


def _wkv_kernel(r_ref, d_ref, k_ref, v_ref, q_ref, p_ref, s0_ref, o_ref, st_ref, m_scr, *, round_state):
    c = pl.program_id(1)
    tc, hc, _ = r_ref.shape
    rnd = _bf16_round if round_state else (lambda z: z)

    @pl.when(c == 0)
    def _():
        m_scr[...] = s0_ref[...]

    def row(ref, t, k):
        return ref[t, k:k + 1, :]

    u0 = None
    for k in range(hc):
        term = rnd(m_scr[k]) * rnd(row(q_ref, 0, k))
        u0 = term if u0 is None else u0 + term

    def step(t, u):
        t_next = jnp.minimum(t + 1, tc - 1)
        v_t = v_ref[t]
        o = u_next = None
        for k in range(hc):
            m_k = m_scr[k] * row(d_ref, t, k) - u * row(p_ref, t, k) + v_t * row(k_ref, t, k)
            m_scr[k] = m_k
            m_b = rnd(m_k)
            t_o = m_b * rnd(row(r_ref, t, k))
            t_u = m_b * rnd(row(q_ref, t_next, k))
            o = t_o if o is None else o + t_o
            u_next = t_u if u_next is None else u_next + t_u
        o_ref[t] = o
        return u_next

    lax.fori_loop(0, tc, step, u0)

    @pl.when(c == pl.num_programs(1) - 1)
    def _():
        st_ref[...] = m_scr[...]


def _wkv(r, d, k, v, q, p, s0, *, hc, round_state):
    nseq, t, dm = r.shape
    n_heads = dm // hc
    n_prob = nseq * n_heads
    lanes = _tile(n_prob, LANES, 1)
    tc = _tile(t, 64, 1)
    to_lanes = lambda z: z.reshape(nseq, t, n_heads, hc).transpose(1, 3, 0, 2).reshape(t, hc, n_prob)
    seq_spec = pl.BlockSpec((tc, hc, lanes), lambda g, c: (c, 0, g))
    st_spec = pl.BlockSpec((hc, hc, lanes), lambda g, c: (0, 0, g))
    o, st = pl.pallas_call(
        functools.partial(_wkv_kernel, round_state=round_state),
        grid=(n_prob // lanes, t // tc),
        in_specs=[seq_spec] * 6 + [st_spec],
        out_specs=[seq_spec, st_spec],
        out_shape=[jax.ShapeDtypeStruct((t, hc, n_prob), F32), jax.ShapeDtypeStruct((hc, hc, n_prob), F32)],
        scratch_shapes=[pltpu.VMEM((hc, hc, lanes), F32)],
        compiler_params=_cparams("arbitrary", "arbitrary"),
    )(*[to_lanes(z) for z in (r, d, k, v, q, p)], s0.transpose(3, 2, 0, 1).reshape(hc, hc, n_prob))
    o = o.reshape(t, hc, nseq, n_heads).transpose(2, 0, 3, 1).reshape(nseq, t, dm)
    st = st.reshape(hc, hc, nseq, n_heads).transpose(2, 3, 1, 0)
    return o, st


def _router_kernel(x_ref, w_ref, idx_ref, gate_ref, *, n_exp):
    logits = jnp.dot(x_ref[...].astype(BF16), w_ref[...].astype(BF16), preferred_element_type=F32)
    lane_i = lax.broadcasted_iota(jnp.int32, logits.shape, 1)
    lane = lane_i.astype(F32)
    logits = jnp.where(lane_i < n_exp, logits, NEG_INF)
    m1 = jnp.max(logits, axis=-1, keepdims=True)
    i1 = jnp.min(jnp.where(logits == m1, lane, float(LANES)), axis=-1, keepdims=True)
    rest = jnp.where(lane == i1, NEG_INF, logits)
    m2 = jnp.max(rest, axis=-1, keepdims=True)
    i2 = jnp.min(jnp.where(rest == m2, lane, float(LANES)), axis=-1, keepdims=True)
    e2 = jnp.exp(m2 - m1)
    den = 1.0 + e2
    idx_ref[...] = jnp.where(lane_i == 0, i1, i2).astype(jnp.int32)
    gate_ref[...] = jnp.where(lane == 0, 1.0 / den, e2 / den)


def _router(x, w_router):
    m, d = x.shape
    n_exp = w_router.shape[1]
    w = jnp.pad(w_router, ((0, 0), (0, LANES - n_exp)))
    tm = _tile(m, 512, 8)
    out = pl.BlockSpec((tm, LANES), lambda i: (i, 0))
    idx, gate = pl.pallas_call(
        functools.partial(_router_kernel, n_exp=n_exp),
        grid=(m // tm,),
        in_specs=[pl.BlockSpec((tm, d), lambda i: (i, 0)), pl.BlockSpec((d, LANES), lambda i: (0, 0))],
        out_specs=[out, out],
        out_shape=[jax.ShapeDtypeStruct((m, LANES), jnp.int32), jax.ShapeDtypeStruct((m, LANES), F32)],
        compiler_params=_cparams("arbitrary"),
    )(x, w)
    return idx[:, :TOP_K], gate[:, :TOP_K]


def _gather_kernel(src_ref, x_hbm, o_ref, buf, sem):
    tr = o_ref.shape[0]
    base = pl.program_id(0) * tr

    def row_copy(i):
        return pltpu.make_async_copy(x_hbm.at[pl.ds(src_ref[base + i], 1)], buf.at[pl.ds(i, 1)], sem)

    def start(i, c):
        row_copy(i).start()
        return c

    def wait(i, c):
        row_copy(i).wait()
        return c

    lax.fori_loop(0, tr, start, 0, unroll=8)
    lax.fori_loop(0, tr, wait, 0, unroll=8)
    o_ref[...] = buf[...].astype(o_ref.dtype)


def _gather_rows(x, src, *, tr):
    rows = src.shape[0]
    d = x.shape[1]
    return pl.pallas_call(
        _gather_kernel,
        grid_spec=pltpu.PrefetchScalarGridSpec(
            num_scalar_prefetch=1,
            grid=(rows // tr,),
            in_specs=[pl.BlockSpec(memory_space=pl.ANY)],
            out_specs=pl.BlockSpec((tr, d), lambda i, src: (i, 0)),
            scratch_shapes=[pltpu.VMEM((tr, d), F32), pltpu.SemaphoreType.DMA(())],
        ),
        out_shape=jax.ShapeDtypeStruct((rows, d), BF16),
        compiler_params=_cparams("arbitrary"),
    )(src, x)


def _combine_ln_kernel(pos_ref, y_hbm, x_ref, gate_ref, g_ref, b_ref, o_ref, buf, sem, *, alpha):
    tr = o_ref.shape[0]
    base = pl.program_id(0) * tr

    def row_copy(i, slot):
        src = pos_ref[(base + i) * TOP_K + slot]
        return pltpu.make_async_copy(y_hbm.at[pl.ds(src, 1)], buf.at[slot, pl.ds(i, 1)], sem)

    def start(i, c):
        for slot in range(TOP_K):
            row_copy(i, slot).start()
        return c

    def wait(i, c):
        for slot in range(TOP_K):
            row_copy(i, slot).wait()
        return c

    lax.fori_loop(0, tr, start, 0, unroll=8)
    lax.fori_loop(0, tr, wait, 0, unroll=8)
    gates = gate_ref[...]
    f = gates[:, 0:1] * buf[0] + gates[:, 1:2] * buf[1]
    o_ref[...] = _ln_rows(alpha * x_ref[...] + f, g_ref[...], b_ref[...])


def _combine_ln(y_sorted, pos, gates, x, g, b, alpha):
    m, d = x.shape
    tr = _tile(m, 128, 8)
    row = pl.BlockSpec((tr, d), lambda i, pos: (i, 0))
    vec = pl.BlockSpec((1, d), lambda i, pos: (0, 0))
    return pl.pallas_call(
        functools.partial(_combine_ln_kernel, alpha=alpha),
        grid_spec=pltpu.PrefetchScalarGridSpec(
            num_scalar_prefetch=1,
            grid=(m // tr,),
            in_specs=[pl.BlockSpec(memory_space=pl.ANY), row,
                      pl.BlockSpec((tr, TOP_K), lambda i, pos: (i, 0)), vec, vec],
            out_specs=row,
            scratch_shapes=[pltpu.VMEM((TOP_K, tr, d), F32), pltpu.SemaphoreType.DMA(())],
        ),
        out_shape=jax.ShapeDtypeStruct((m, d), F32),
        compiler_params=_cparams("arbitrary"),
    )(pos, y_sorted, x, gates, g.reshape(1, d), b.reshape(1, d))


def _route_tables(top_i, n_exp, tm):
    m = top_i.shape[0]
    n_rows = m * TOP_K
    n_tiles = -(-(n_rows + n_exp * (tm - 1)) // tm)
    e_flat = top_i.reshape(-1)
    order = jnp.argsort(e_flat, stable=True)
    counts = jnp.sum(e_flat[:, None] == jnp.arange(n_exp, dtype=jnp.int32)[None, :], axis=0).astype(jnp.int32)
    padded = (counts + tm - 1) // tm * tm
    pad_end = jnp.cumsum(padded)
    pad_start = pad_end - padded
    cnt_start = jnp.cumsum(counts) - counts
    e_sorted = e_flat[order]
    dest_sorted = pad_start[e_sorted] + jnp.arange(n_rows, dtype=jnp.int32) - cnt_start[e_sorted]
    pos = jnp.zeros((n_rows,), jnp.int32).at[order].set(dest_sorted)
    src = jnp.zeros((n_tiles * tm,), jnp.int32).at[dest_sorted].set((order // TOP_K).astype(jnp.int32))
    n_valid = pad_end[-1] // tm
    t_idx = jnp.arange(n_tiles, dtype=jnp.int32)
    t_eff = jnp.minimum(t_idx, jnp.maximum(n_valid - 1, 0))
    te = jnp.searchsorted(pad_end, t_eff * tm, side="right").astype(jnp.int32)
    te = jnp.minimum(te, n_exp - 1)
    tf = jnp.concatenate([jnp.ones((1,), jnp.int32), (te[1:] != te[:-1]).astype(jnp.int32)])
    tv = (t_idx < n_valid).astype(jnp.int32)
    return src, pos, (te, tf, tv, t_eff, t_idx)


def _rope(x, pos, hd):
    half = hd // 2
    inv = jnp.power(ROPE_THETA, -jnp.arange(half, dtype=F32) / half)
    ang = pos.astype(F32)[:, None] * inv[None, :]
    cos, sin = jnp.cos(ang)[:, None, :], jnp.sin(ang)[:, None, :]
    xh = x.reshape(x.shape[0], -1, hd)
    x1, x2 = xh[..., :half], xh[..., half:]
    return jnp.concatenate([x1 * cos - x2 * sin, x2 * cos + x1 * sin], axis=-1).reshape(x.shape)


def _even_mixer(x_bf, dims, cache_k, cache_v, state_conv, w_in, b_in, sinks, conv_w, conv_b, conv_g, conv_bn,
                w_out, b_out):
    nseq, seq, db, ds, n_p = dims
    m = x_bf.shape[0]
    _, w, n_kv, hd = cache_k.shape
    d_kv = n_kv * hd
    d_q = sinks.shape[0] * hd
    d_conv = state_conv.shape[-1]
    cw = conv_w.shape[0]
    tm = _tile(m, 640, BF16_SUBLANES)
    col_a, col_g = d_q + 2 * d_kv, d_q + 2 * d_kv + d_conv

    tm_wide = _tile(m, 416, BF16_SUBLANES)

    def proj(col, width):
        tn = _tile(math.gcd(col, width), 1024, LANES)
        return _dense(x_bf, w_in, tm=tm_wide if tn > 512 else tm, tn=tn, bias=b_in, col=col, n_out=width)

    pos = jnp.concatenate([jnp.tile(jnp.arange(seq, dtype=jnp.int32), nseq),
                           jnp.tile(PAST_LEN + jnp.arange(ds, dtype=jnp.int32), db)])
    q = _rope(proj(0, d_q), pos, hd).astype(BF16)
    if d_kv % LANES == 0:
        k, v = proj(d_q, d_kv), proj(d_q + d_kv, d_kv)
    else:
        kv = proj(d_q, col_a - d_q)
        k, v = kv[:, :d_kv], kv[:, d_kv:]
    k = _rope(k, pos, hd)
    u = _mm(x_bf, [w_in[None], w_in[None]], tm=tm, tn=_tile(math.gcd(col_a, col_g, d_conv), 256, LANES),
            cols=(col_a, col_g), n_out=d_conv, biases=[b_in, b_in], act="glu")
    k_bf, v_bf = k.astype(BF16), v.astype(BF16)

    att_p = _swa_prompt(q[:n_p], k_bf[:n_p], v_bf[:n_p], sinks, nseq=nseq, seq=seq, n_kv=n_kv, hd=hd, window=w)
    cache_k2 = cache_k.reshape(db, w, d_kv)
    cache_v2 = cache_v.reshape(db, w, d_kv)
    att_s = _swa_sample(q[n_p:].reshape(db, ds, d_q), k_bf[n_p:].reshape(db, ds, d_kv),
                        v_bf[n_p:].reshape(db, ds, d_kv), cache_k2, cache_v2, sinks, n_kv=n_kv, hd=hd)
    conv_p = _conv_prompt(u[:n_p], conv_w, conv_b, conv_g, conv_bn, nseq=nseq, seq=seq)
    up_s = jnp.concatenate([state_conv, u[n_p:].reshape(db, ds, d_conv)], axis=1)
    conv_s = _conv_sample(up_s, conv_w, conv_b, conv_g, conv_bn, ds=ds)

    mix = jnp.concatenate([jnp.concatenate([att_p, att_s.reshape(db * ds, d_q)], axis=0),
                           jnp.concatenate([conv_p, conv_s.reshape(db * ds, d_conv)], axis=0)], axis=1)
    y = _dense(mix, w_out, tm=tm_wide, tn=_tile(w_out.shape[1], 1024, LANES), bias=b_out)

    kp = k[:n_p].reshape(nseq, seq, n_kv, hd)[:, -w:]
    vp = v[:n_p].reshape(nseq, seq, n_kv, hd)[:, -w:]
    ks = jnp.concatenate([cache_k, k[n_p:].reshape(db, ds, n_kv, hd)], axis=1)[:, -w:]
    vs = jnp.concatenate([cache_v, v[n_p:].reshape(db, ds, n_kv, hd)], axis=1)[:, -w:]
    cp = u[:n_p].reshape(nseq, seq, d_conv)[:, -(cw - 1):]
    cs = up_s[:, -(cw - 1):]
    return y, (kp, vp, ks, vs, cp, cs)


def _rwkv_mixer(x, dims, state_shift, state_wkv, mu, w0, w1, w2, a0, a1, a2, g1, g2, k_k, k_a, r_k, w_r, w_k, w_v,
                w_o, lnx_g, lnx_b):
    nseq, seq, db, ds, n_p = dims
    m, d = x.shape
    n_heads, hc = state_wkv.shape[1], state_wkv.shape[-1]
    xp = x[:n_p].reshape(nseq, seq, d)
    xs = x[n_p:].reshape(db, ds, d)
    prev_p = jnp.concatenate([jnp.zeros((nseq, 1, d), F32), xp[:, :-1]], axis=1)
    prev_s = jnp.concatenate([state_shift[:, None], xs[:, :-1]], axis=1)
    prev = jnp.concatenate([prev_p.reshape(n_p, d), prev_s.reshape(db * ds, d)], axis=0)
    xx = prev - x
    xr, xw, xk, xv, xa, xg = [(x + xx * mu[i]).astype(BF16) for i in range(6)]
    tm = _tile(m, 416, BF16_SUBLANES)
    big = lambda z, wt: _dense(z, wt, tm=tm, tn=_tile(wt.shape[1], 1024, LANES))
    r = big(xr, w_r)
    k = big(xk, w_k)
    v = big(xv, w_v)
    lw = big(jnp.tanh(big(xw, w1)).astype(BF16), w2)
    la = big(big(xa, a1).astype(BF16), a2)
    g = big(jax.nn.sigmoid(big(xg, g1)).astype(BF16), g2)

    def part(rows, n_s, t, state, round_state):
        r_, k_, v_, lw_, la_, g_ = [z[rows] for z in (r, k, v, lw, la, g)]
        n = n_s * t
        heads = lambda z: z.reshape(n, n_heads, hc)
        w = -jax.nn.softplus(-(w0 + lw_)) - 0.5
        a = jax.nn.sigmoid(a0 + la_)
        kk = heads(k_ * k_k)
        kk = (kk * lax.rsqrt(jnp.maximum(jnp.sum(kk * kk, axis=-1, keepdims=True), 1e-24))).reshape(n, d)
        k_ = k_ * (1 + (a - 1) * k_a)
        decay = jnp.exp(-jnp.exp(w))
        wkv_in = [z.reshape(n_s, t, d) for z in (r_, decay, k_, v_, kk, kk * a)]
        o, st = _wkv(*wkv_in, state, hc=hc, round_state=round_state)
        o = heads(o.reshape(n, d))
        o_mean = jnp.mean(o, axis=-1, keepdims=True)
        o_var = jnp.mean(jnp.square(o - o_mean), axis=-1, keepdims=True)
        o = ((o - o_mean) * lax.rsqrt(o_var + GN_EPS)).reshape(n, d) * lnx_g + lnx_b
        bonus = jnp.sum(heads(r_) * heads(k_) * r_k, axis=-1, keepdims=True) * heads(v_)
        return ((o + bonus.reshape(n, d)) * g_).astype(BF16), st

    og_p, wkv_p = part(slice(0, n_p), nseq, seq, jnp.zeros((nseq, n_heads, hc, hc), F32), False)
    og_s, wkv_s = part(slice(n_p, m), db, ds, state_wkv, True)
    y = big(jnp.concatenate([og_p, og_s], axis=0), w_o)
    return y, (xp[:, -1], xs[:, -1], wkv_p, wkv_s)


def _swiglu_dense(x_bf, wg, wu, wd, *, tm):
    d_ff = wg.shape[1]
    h = _mm(x_bf, [wg[None], wu[None]], tm=_tile(x_bf.shape[0], 832, BF16_SUBLANES), tn=_tile(d_ff, 256, LANES),
            act="swiglu", out_dtype=BF16)
    nk = 2 if d_ff % (2 * LANES) == 0 and d_ff > 8192 else 1
    return _dense(h, wd, tm=tm, tn=_tile(wd.shape[1], 512, LANES), nk=nk)


def _moe(x, x_ln_args, w_router, wg, wu, wd, alpha):
    m, d = x.shape
    n_exp, _, d_ff = wg.shape
    tm = 512 if m * TOP_K >= 4096 else 64
    top_i, gates = _router(x, w_router)
    src, pos, tiles = _route_tables(top_i, n_exp, tm)
    xs = _gather_rows(x, src, tr=_tile(src.shape[0], 256, BF16_SUBLANES))
    h = _mm(xs, [wg, wu], tm=tm, tn=_tile(d_ff, 512, LANES), act="swiglu", out_dtype=BF16, tiles=tiles)
    nk = 4 if d_ff % (4 * LANES) == 0 and d_ff > 8192 else 1
    y = None
    for kk in range(nk):
        y = _mm(h, [wd], tm=tm, tn=_tile(d, 512, LANES), nk=nk, kk=kk, add=y, tiles=tiles)
    g, b = x_ln_args
    return _combine_ln(y, pos, gates, x, g, b, alpha)


def kernel(x_prompt, x_sample, cache_swa_k, cache_swa_v, state_conv, state_shift, state_wkv, ln_g, ln_b, w_in, b_in,
           sinks, conv_w, conv_b, conv_ln_g, conv_ln_b, w_out, b_out, ffn_w_gate, ffn_w_up, ffn_w_down, mu, w0, w1,
           w2, a0, a1, a2, g1, g2, k_k, k_a, r_k, w_r, w_k, w_v, w_o, lnx_g, lnx_b, w_router, moe_w_gate, moe_w_up,
           moe_w_down):
    nseq, seq, d = x_prompt.shape
    db, ds, _ = x_sample.shape
    n_p = nseq * seq
    dims = (nseq, seq, db, ds, n_p)
    depth = ln_g.shape[0]
    alpha = (2 * depth) ** 0.25
    x = jnp.concatenate([x_prompt.reshape(n_p, d), x_sample.reshape(db * ds, d)], axis=0)
    m = x.shape[0]
    tm = _tile(m, 640, BF16_SUBLANES)
    x_bf = x.astype(BF16)
    even_out, odd_out = [], []
    for layer in range(depth):
        i = layer // 2
        if layer % 2 == 0:
            y, states = _even_mixer(x_bf, dims, cache_swa_k[i], cache_swa_v[i], state_conv[i], w_in[i], b_in[i],
                                    sinks[i], conv_w[i], conv_b[i], conv_ln_g[i], conv_ln_b[i], w_out[i], b_out[i])
            even_out.append(states)
        else:
            y, states = _rwkv_mixer(x, dims, state_shift[i], state_wkv[i], mu[i], w0[i], w1[i], w2[i], a0[i], a1[i],
                                    a2[i], g1[i], g2[i], k_k[i], k_a[i], r_k[i], w_r[i], w_k[i], w_v[i], w_o[i],
                                    lnx_g[i], lnx_b[i])
            odd_out.append(states)
        x, x_bf = _add_ln(x, y, ln_g[layer, 0], ln_b[layer, 0], alpha)
        if layer % 2 == 0:
            f = _swiglu_dense(x_bf, ffn_w_gate[i], ffn_w_up[i], ffn_w_down[i], tm=tm)
            x, x_bf = _add_ln(x, f, ln_g[layer, 1], ln_b[layer, 1], alpha)
        else:
            x = _moe(x, (ln_g[layer, 1], ln_b[layer, 1]), w_router[i], moe_w_gate[i], moe_w_up[i], moe_w_down[i],
                     alpha)
            x_bf = x.astype(BF16)
    stack = lambda outs, j: jnp.stack([o[j] for o in outs])
    return (x[:n_p].reshape(nseq, seq, d), x[n_p:].reshape(db, ds, d),
            stack(even_out, 0), stack(even_out, 1), stack(even_out, 2), stack(even_out, 3),
            stack(even_out, 4), stack(even_out, 5),
            stack(odd_out, 0), stack(odd_out, 1), stack(odd_out, 2), stack(odd_out, 3))
```

```python
import functools
import math

import jax
import jax.numpy as jnp
from jax import lax
from jax.experimental import pallas as pl
from jax.experimental.pallas import tpu as pltpu

F32 = jnp.float32
BF16 = jnp.bfloat16

PAST_LEN = 16384
ROPE_THETA = 10000.0
LN_EPS = 1e-5
GN_EPS = 64e-5
TOP_K = 2
NEG_INF = float("-inf")

VMEM_LIMIT_BYTES = 58 * 1024 * 1024
LANES = 128
BF16_SUBLANES = 16
DMA_PRIORITIES = 2


def _cparams(*sem):
    return pltpu.CompilerParams(dimension_semantics=sem, vmem_limit_bytes=VMEM_LIMIT_BYTES)


def _tile(n, target, mult):
    best = None
    for t in range(mult, min(n, target) + 1, mult):
        if n % t == 0:
            best = t
    return best if best is not None else n


def _mm_kernel(te_ref, tf_ref, tv_ref, tx_ref, to_ref, *refs, n_w, n_bias, has_add, act):
    del te_ref, tx_ref, to_ref
    x_ref = refs[0]
    w_refs = refs[1:1 + n_w]
    b_refs = refs[1 + n_w:1 + n_w + n_bias]
    pos = 1 + n_w + n_bias
    a_ref = None
    if has_add:
        a_ref = refs[pos]
        pos += 1
    o_ref = refs[pos]
    wb_refs = refs[pos + 1:pos + 1 + n_w]
    t = pl.program_id(1)

    @pl.when(tf_ref[t] == 1)
    def _():
        for w_ref, wb_ref in zip(w_refs, wb_refs):
            wb_ref[...] = w_ref[0].astype(BF16)

    @pl.when(tv_ref[t] == 1)
    def _():
        x = x_ref[...]
        accs = [jnp.dot(x, wb_ref[...], preferred_element_type=F32) for wb_ref in wb_refs]
        if n_bias:
            accs = [acc + b_ref[...] for acc, b_ref in zip(accs, b_refs)]
        if act == "swiglu":
            acc = accs[0] * jax.nn.sigmoid(accs[0]) * accs[1]
        elif act == "glu":
            acc = accs[0] * jax.nn.sigmoid(accs[1])
        else:
            acc = accs[0]
        if has_add:
            acc = acc + a_ref[...]
        o_ref[...] = acc.astype(o_ref.dtype)

    @pl.when(tv_ref[t] == 0)
    def _():
        o_ref[...] = jnp.zeros(o_ref.shape, o_ref.dtype)


def _mm(x, ws, *, tm, tn, nk=1, kk=0, cols=None, n_out=None, biases=None, add=None, act=None, out_dtype=F32,
        tiles=None):
    m, k = x.shape
    n = ws[0].shape[-1]
    n_out = n if n_out is None else n_out
    cols = (0,) * len(ws) if cols is None else cols
    assert m % tm == 0 and n_out % tn == 0 and k % nk == 0 and all(c % tn == 0 for c in cols)
    tk = k // nk
    nt = m // tm
    if tiles is None:
        idx = jnp.arange(nt, dtype=jnp.int32)
        tiles = (jnp.zeros((nt,), jnp.int32), (idx == 0).astype(jnp.int32), jnp.ones((nt,), jnp.int32), idx, idx)
    in_specs = [pl.BlockSpec((tm, tk), lambda j, t, te, tf, tv, tx, to: (tx[t], kk))]
    args = [x]
    for w, c in zip(ws, cols):
        in_specs.append(pl.BlockSpec((1, tk, tn), lambda j, t, te, tf, tv, tx, to, c=c: (te[t], kk, c // tn + j)))
        args.append(w)
    for b, c in zip(biases or (), cols):
        in_specs.append(pl.BlockSpec((1, tn), lambda j, t, te, tf, tv, tx, to, c=c: (0, c // tn + j)))
        args.append(b.reshape(1, n))
    out_spec = pl.BlockSpec((tm, tn), lambda j, t, te, tf, tv, tx, to: (to[t], j))
    if add is not None:
        in_specs.append(out_spec)
        args.append(add)
    kern = functools.partial(_mm_kernel, n_w=len(ws), n_bias=len(biases or ()), has_add=add is not None, act=act)
    return pl.pallas_call(
        kern,
        grid_spec=pltpu.PrefetchScalarGridSpec(
            num_scalar_prefetch=5,
            grid=(n_out // tn, nt),
            in_specs=in_specs,
            out_specs=out_spec,
            scratch_shapes=[pltpu.VMEM((tk, tn), BF16) for _ in ws],
        ),
        out_shape=jax.ShapeDtypeStruct((m, n_out), out_dtype),
        compiler_params=_cparams("arbitrary", "arbitrary"),
    )(*tiles, *args)


def _dense(x, w, *, tm, tn, nk=1, bias=None, col=0, n_out=None, out_dtype=F32):
    out = None
    for kk in range(nk):
        last = kk == nk - 1
        out = _mm(x, [w[None]], tm=tm, tn=tn, nk=nk, kk=kk, cols=(col,), n_out=n_out,
                  biases=[bias] if bias is not None and kk == 0 else None, add=out,
                  out_dtype=out_dtype if last else F32)
    return out


def _ln_rows(z, g, b):
    mean = jnp.mean(z, axis=-1, keepdims=True)
    zc = z - mean
    var = jnp.mean(zc * zc, axis=-1, keepdims=True)
    return zc * lax.rsqrt(var + LN_EPS) * g + b


def _add_ln_kernel(x_ref, y_ref, g_ref, b_ref, o_ref, ob_ref, *, alpha):
    out = _ln_rows(alpha * x_ref[...] + y_ref[...], g_ref[...], b_ref[...])
    o_ref[...] = out
    ob_ref[...] = out.astype(BF16)


def _add_ln(x, y, g, b, alpha):
    m, d = x.shape
    tm = _tile(m, 256, BF16_SUBLANES)
    row = pl.BlockSpec((tm, d), lambda i: (i, 0))
    vec = pl.BlockSpec((1, d), lambda i: (0, 0))
    return pl.pallas_call(
        functools.partial(_add_ln_kernel, alpha=alpha),
        grid=(m // tm,),
        in_specs=[row, row, vec, vec],
        out_specs=[row, row],
        out_shape=[jax.ShapeDtypeStruct((m, d), F32), jax.ShapeDtypeStruct((m, d), BF16)],
        compiler_params=_cparams("arbitrary"),
    )(x, y, g.reshape(1, d), b.reshape(1, d))


def _sink_softmax_pv(pieces, sink):
    m = sink
    for s, _ in pieces:
        m = jnp.maximum(jnp.max(s, axis=-1, keepdims=True), m)
    ps = [jnp.exp(s - m) for s, _ in pieces]
    den = jnp.exp(sink - m)
    for p in ps:
        den = den + jnp.sum(p, axis=-1, keepdims=True)
    out = None
    for p, (_, v) in zip(ps, pieces):
        o = jnp.dot((p / den).astype(BF16), v, preferred_element_type=F32)
        out = o if out is None else out + o
    return out


def _qk(q, k):
    return lax.dot_general(q, k, (((1,), (1,)), ((), ())), preferred_element_type=F32)


def _swa_prompt_kernel(sink_ref, q_ref, kp_ref, kc_ref, vp_ref, vc_ref, o_ref, *, n_kv, group, hd, window):
    i = pl.program_id(1)
    blk = q_ref.shape[0]
    kcat = jnp.concatenate([kp_ref[...], kc_ref[...]], axis=0)
    vcat = jnp.concatenate([vp_ref[...], vc_ref[...]], axis=0)
    row = lax.broadcasted_iota(jnp.int32, (blk, 2 * blk), 0)
    col = lax.broadcasted_iota(jnp.int32, (blk, 2 * blk), 1)
    diff = row + blk - col
    mask = (diff >= 0) & (diff <= window) & ((col >= blk) | (i > 0))
    scale = hd ** -0.5
    for g in range(n_kv):
        kg = kcat[:, g * hd:(g + 1) * hd]
        vg = vcat[:, g * hd:(g + 1) * hd]
        for j in range(group):
            h = g * group + j
            s = _qk(q_ref[:, h * hd:(h + 1) * hd], kg) * scale
            s = jnp.where(mask, s, NEG_INF)
            o = _sink_softmax_pv([(s, vg)], sink_ref[h])
            o_ref[:, h * hd:(h + 1) * hd] = o.astype(o_ref.dtype)


def _swa_prompt(q, k, v, sinks, *, nseq, seq, n_kv, hd, window):
    blk = window
    nb = seq // blk
    d_q, d_kv = q.shape[1], k.shape[1]
    group = d_q // hd // n_kv
    cur = lambda b, i: (b * nb + i, 0)
    prev = lambda b, i: (b * nb + jnp.maximum(i - 1, 0), 0)
    return pl.pallas_call(
        functools.partial(_swa_prompt_kernel, n_kv=n_kv, group=group, hd=hd, window=window),
        grid=(nseq, nb),
        in_specs=[pl.BlockSpec(memory_space=pltpu.SMEM),
                  pl.BlockSpec((blk, d_q), cur),
                  pl.BlockSpec((blk, d_kv), prev), pl.BlockSpec((blk, d_kv), cur),
                  pl.BlockSpec((blk, d_kv), prev), pl.BlockSpec((blk, d_kv), cur)],
        out_specs=pl.BlockSpec((blk, d_q), cur),
        out_shape=jax.ShapeDtypeStruct(q.shape, BF16),
        compiler_params=_cparams("arbitrary", "arbitrary"),
    )(sinks, q, k, k, v, v)


def _swa_sample_kernel(sink_ref, q_ref, kn_ref, vn_ref, kc_ref, vc_ref, o_ref, *, n_kv, group, hd):
    ds = kn_ref.shape[1]
    w = kc_ref.shape[1]
    rows = group * ds
    kc = kc_ref[0].astype(BF16)
    vc = vc_ref[0].astype(BF16)
    kn = kn_ref[0]
    vn = vn_ref[0]
    row = lax.broadcasted_iota(jnp.int32, (rows, 1), 0)
    q_head = jnp.zeros((rows, 1), jnp.int32)
    for j in range(1, group):
        q_head = jnp.where(row >= j * ds, j, q_head)
    q_pos = row - q_head * ds
    mask_c = lax.broadcasted_iota(jnp.int32, (rows, w), 1) >= q_pos
    mask_n = lax.broadcasted_iota(jnp.int32, (rows, ds), 1) <= q_pos
    scale = hd ** -0.5
    for g in range(n_kv):
        sl = slice(g * hd, (g + 1) * hd)
        sink = jnp.zeros((rows, 1), F32)
        for j in range(group):
            sink = jnp.where(q_head == j, sink_ref[g * group + j], sink)
        qg = q_ref[0, g]
        s_c = jnp.where(mask_c, _qk(qg, kc[:, sl]) * scale, NEG_INF)
        s_n = jnp.where(mask_n, _qk(qg, kn[:, sl]) * scale, NEG_INF)
        o = _sink_softmax_pv([(s_c, vc[:, sl]), (s_n, vn[:, sl])], sink)
        o_ref[0, g] = o.astype(o_ref.dtype)


def _swa_sample(q, kn, vn, kc, vc, sinks, *, n_kv, hd):
    db, ds, d_q = q.shape
    d_kv = kn.shape[2]
    w = kc.shape[1]
    group = d_q // hd // n_kv
    qg = q.reshape(db, ds, n_kv, group, hd).transpose(0, 2, 3, 1, 4).reshape(db, n_kv, group * ds, hd)
    b3 = lambda shape: pl.BlockSpec((1,) + shape, lambda b: (b, 0, 0))
    b4 = pl.BlockSpec((1, n_kv, group * ds, hd), lambda b: (b, 0, 0, 0))
    og = pl.pallas_call(
        functools.partial(_swa_sample_kernel, n_kv=n_kv, group=group, hd=hd),
        grid=(db,),
        in_specs=[pl.BlockSpec(memory_space=pltpu.SMEM), b4, b3((ds, d_kv)), b3((ds, d_kv)),
                  b3((w, d_kv)), b3((w, d_kv))],
        out_specs=b4,
        out_shape=jax.ShapeDtypeStruct(qg.shape, BF16),
        compiler_params=_cparams("arbitrary"),
    )(sinks, qg, kn, vn, kc, vc)
    return og.reshape(db, n_kv, group, ds, hd).transpose(0, 3, 1, 2, 4).reshape(db, ds, d_q)


def _bf16_round(x):
    return x.astype(BF16).astype(F32)


def _conv_tail(y, cb_ref, g_ref, b_ref):
    z = _ln_rows(y + cb_ref[...], g_ref[...], b_ref[...])
    return z * jax.nn.sigmoid(z)


def _conv_prompt_kernel(up_ref, uc_ref, w_ref, cb_ref, g_ref, b_ref, o_ref, *, conv_w, halo):
    i = pl.program_id(1)
    tq = uc_ref.shape[0]
    prev = jnp.where(i > 0, up_ref[tq - halo:, :], 0.0)
    up = _bf16_round(jnp.concatenate([prev, uc_ref[...]], axis=0))
    w = w_ref[...]
    off = halo - (conv_w - 1)
    y = None
    for j in range(conv_w):
        term = up[off + j:off + j + tq, :] * w[j:j + 1, :]
        y = term if y is None else y + term
    o_ref[...] = _conv_tail(y, cb_ref, g_ref, b_ref).astype(o_ref.dtype)


def _conv_prompt(u, conv_w, conv_b, ln_g, ln_b, *, nseq, seq):
    c = u.shape[1]
    cw = conv_w.shape[0]
    halo = -(-(cw - 1) // 8) * 8
    tq = _tile(seq, 256, BF16_SUBLANES)
    nb = seq // tq
    cur = lambda b, i: (b * nb + i, 0)
    prev = lambda b, i: (b * nb + jnp.maximum(i - 1, 0), 0)
    vec = pl.BlockSpec((1, c), lambda b, i: (0, 0))
    return pl.pallas_call(
        functools.partial(_conv_prompt_kernel, conv_w=cw, halo=halo),
        grid=(nseq, nb),
        in_specs=[pl.BlockSpec((tq, c), prev), pl.BlockSpec((tq, c), cur),
                  pl.BlockSpec((cw, c), lambda b, i: (0, 0)), vec, vec, vec],
        out_specs=pl.BlockSpec((tq, c), cur),
        out_shape=jax.ShapeDtypeStruct(u.shape, BF16),
        compiler_params=_cparams("arbitrary", "arbitrary"),
    )(u, u, conv_w, conv_b.reshape(1, c), ln_g.reshape(1, c), ln_b.reshape(1, c))


def _conv_sample_kernel(up_ref, w_ref, cb_ref, g_ref, b_ref, o_ref, *, conv_w):
    ds = o_ref.shape[1]
    up = _bf16_round(up_ref[0])
    w = w_ref[...]
    y = None
    for j in range(conv_w):
        term = up[j:j + ds, :] * w[j:j + 1, :]
        y = term if y is None else y + term
    o_ref[0] = _conv_tail(y, cb_ref, g_ref, b_ref).astype(o_ref.dtype)


def _conv_sample(up, conv_w, conv_b, ln_g, ln_b, *, ds):
    db, length, c = up.shape
    cw = conv_w.shape[0]
    vec = pl.BlockSpec((1, c), lambda b: (0, 0))
    return pl.pallas_call(
        functools.partial(_conv_sample_kernel, conv_w=cw),
        grid=(db,),
        in_specs=[pl.BlockSpec((1, length, c), lambda b: (b, 0, 0)), pl.BlockSpec((cw, c), lambda b: (0, 0)),
                  vec, vec, vec],
        out_specs=pl.BlockSpec((1, ds, c), lambda b: (b, 0, 0)),
        out_shape=jax.ShapeDtypeStruct((db, ds, c), BF16),
        compiler_params=_cparams("arbitrary"),
    )(up, conv_w, conv_b.reshape(1, c), ln_g.reshape(1, c), ln_b.reshape(1, c))


def _wkv_kernel(r_ref, d_ref, k_ref, v_ref, q_ref, p_ref, s0_ref, o_ref, st_ref, m_scr, *, round_state):
    c = pl.program_id(1)
    tc, hc, _ = r_ref.shape
    rnd = _bf16_round if round_state else (lambda z: z)

    @pl.when(c == 0)
    def _():
        m_scr[...] = s0_ref[...]

    def row(ref, t, k):
        return ref[t, k:k + 1, :]

    u0 = None
    for k in range(hc):
        term = rnd(m_scr[k]) * rnd(row(q_ref, 0, k))
        u0 = term if u0 is None else u0 + term

    def step(t, u):
        t_next = jnp.minimum(t + 1, tc - 1)
        v_t = v_ref[t]
        o = u_next = None
        for k in range(hc):
            m_k = m_scr[k] * row(d_ref, t, k) - u * row(p_ref, t, k) + v_t * row(k_ref, t, k)
            m_scr[k] = m_k
            m_b = rnd(m_k)
            t_o = m_b * rnd(row(r_ref, t, k))
            t_u = m_b * rnd(row(q_ref, t_next, k))
            o = t_o if o is None else o + t_o
            u_next = t_u if u_next is None else u_next + t_u
        o_ref[t] = o
        return u_next

    lax.fori_loop(0, tc, step, u0)

    @pl.when(c == pl.num_programs(1) - 1)
    def _():
        st_ref[...] = m_scr[...]


def _wkv(r, d, k, v, q, p, s0, *, hc, round_state):
    nseq, t, dm = r.shape
    n_heads = dm // hc
    n_prob = nseq * n_heads
    lanes = _tile(n_prob, LANES, 1)
    tc = _tile(t, 64, 1)
    to_lanes = lambda z: z.reshape(nseq, t, n_heads, hc).transpose(1, 3, 0, 2).reshape(t, hc, n_prob)
    seq_spec = pl.BlockSpec((tc, hc, lanes), lambda g, c: (c, 0, g))
    st_spec = pl.BlockSpec((hc, hc, lanes), lambda g, c: (0, 0, g))
    o, st = pl.pallas_call(
        functools.partial(_wkv_kernel, round_state=round_state),
        grid=(n_prob // lanes, t // tc),
        in_specs=[seq_spec] * 6 + [st_spec],
        out_specs=[seq_spec, st_spec],
        out_shape=[jax.ShapeDtypeStruct((t, hc, n_prob), F32), jax.ShapeDtypeStruct((hc, hc, n_prob), F32)],
        scratch_shapes=[pltpu.VMEM((hc, hc, lanes), F32)],
        compiler_params=_cparams("arbitrary", "arbitrary"),
    )(*[to_lanes(z) for z in (r, d, k, v, q, p)], s0.transpose(3, 2, 0, 1).reshape(hc, hc, n_prob))
    o = o.reshape(t, hc, nseq, n_heads).transpose(2, 0, 3, 1).reshape(nseq, t, dm)
    st = st.reshape(hc, hc, nseq, n_heads).transpose(2, 3, 1, 0)
    return o, st


def _router_kernel(x_ref, w_ref, idx_ref, gate_ref, *, n_exp):
    logits = jnp.dot(x_ref[...].astype(BF16), w_ref[...].astype(BF16), preferred_element_type=F32)
    lane_i = lax.broadcasted_iota(jnp.int32, logits.shape, 1)
    lane = lane_i.astype(F32)
    logits = jnp.where(lane_i < n_exp, logits, NEG_INF)
    m1 = jnp.max(logits, axis=-1, keepdims=True)
    i1 = jnp.min(jnp.where(logits == m1, lane, float(LANES)), axis=-1, keepdims=True)
    rest = jnp.where(lane == i1, NEG_INF, logits)
    m2 = jnp.max(rest, axis=-1, keepdims=True)
    i2 = jnp.min(jnp.where(rest == m2, lane, float(LANES)), axis=-1, keepdims=True)
    e2 = jnp.exp(m2 - m1)
    den = 1.0 + e2
    idx_ref[...] = jnp.where(lane_i == 0, i1, i2).astype(jnp.int32)
    gate_ref[...] = jnp.where(lane == 0, 1.0 / den, e2 / den)


def _router(x, w_router):
    m, d = x.shape
    n_exp = w_router.shape[1]
    w = jnp.pad(w_router, ((0, 0), (0, LANES - n_exp)))
    tm = _tile(m, 512, 8)
    out = pl.BlockSpec((tm, LANES), lambda i: (i, 0))
    idx, gate = pl.pallas_call(
        functools.partial(_router_kernel, n_exp=n_exp),
        grid=(m // tm,),
        in_specs=[pl.BlockSpec((tm, d), lambda i: (i, 0)), pl.BlockSpec((d, LANES), lambda i: (0, 0))],
        out_specs=[out, out],
        out_shape=[jax.ShapeDtypeStruct((m, LANES), jnp.int32), jax.ShapeDtypeStruct((m, LANES), F32)],
        compiler_params=_cparams("arbitrary"),
    )(x, w)
    return idx[:, :TOP_K], gate[:, :TOP_K]


def _gather_kernel(src_ref, x_hbm, o_ref, buf, sem):
    tr = o_ref.shape[0]
    base = pl.program_id(0) * tr

    def row_copy(i):
        return pltpu.make_async_copy(x_hbm.at[pl.ds(src_ref[base + i], 1)], buf.at[pl.ds(i, 1)], sem)

    def start(i, c):
        for lane in range(DMA_PRIORITIES):
            row_copy(i * DMA_PRIORITIES + lane).start(priority=lane)
        return c

    def wait(i, c):
        row_copy(i).wait()
        return c

    lax.fori_loop(0, tr // DMA_PRIORITIES, start, 0, unroll=4)
    lax.fori_loop(0, tr, wait, 0, unroll=8)
    o_ref[...] = buf[...].astype(o_ref.dtype)


def _gather_rows(x, src, *, tr):
    rows = src.shape[0]
    d = x.shape[1]
    return pl.pallas_call(
        _gather_kernel,
        grid_spec=pltpu.PrefetchScalarGridSpec(
            num_scalar_prefetch=1,
            grid=(rows // tr,),
            in_specs=[pl.BlockSpec(memory_space=pl.ANY)],
            out_specs=pl.BlockSpec((tr, d), lambda i, src: (i, 0)),
            scratch_shapes=[pltpu.VMEM((tr, d), F32), pltpu.SemaphoreType.DMA(())],
        ),
        out_shape=jax.ShapeDtypeStruct((rows, d), BF16),
        compiler_params=_cparams("arbitrary"),
    )(src, x)


def _combine_ln_kernel(pos_ref, y_hbm, x_ref, gate_ref, g_ref, b_ref, o_ref, buf, sem, *, alpha):
    tr = o_ref.shape[0]
    base = pl.program_id(0) * tr

    def row_copy(i, slot):
        src = pos_ref[(base + i) * TOP_K + slot]
        return pltpu.make_async_copy(y_hbm.at[pl.ds(src, 1)], buf.at[slot, pl.ds(i, 1)], sem)

    def start(i, c):
        for slot in range(TOP_K):
            row_copy(i, slot).start(priority=slot % DMA_PRIORITIES)
        return c

    def wait(i, c):
        for slot in range(TOP_K):
            row_copy(i, slot).wait()
        return c

    lax.fori_loop(0, tr, start, 0, unroll=8)
    lax.fori_loop(0, tr, wait, 0, unroll=8)
    gates = gate_ref[...]
    f = gates[:, 0:1] * buf[0] + gates[:, 1:2] * buf[1]
    o_ref[...] = _ln_rows(alpha * x_ref[...] + f, g_ref[...], b_ref[...])


def _combine_ln(y_sorted, pos, gates, x, g, b, alpha):
    m, d = x.shape
    tr = _tile(m, 128, 8)
    row = pl.BlockSpec((tr, d), lambda i, pos: (i, 0))
    vec = pl.BlockSpec((1, d), lambda i, pos: (0, 0))
    return pl.pallas_call(
        functools.partial(_combine_ln_kernel, alpha=alpha),
        grid_spec=pltpu.PrefetchScalarGridSpec(
            num_scalar_prefetch=1,
            grid=(m // tr,),
            in_specs=[pl.BlockSpec(memory_space=pl.ANY), row,
                      pl.BlockSpec((tr, TOP_K), lambda i, pos: (i, 0)), vec, vec],
            out_specs=row,
            scratch_shapes=[pltpu.VMEM((TOP_K, tr, d), F32), pltpu.SemaphoreType.DMA(())],
        ),
        out_shape=jax.ShapeDtypeStruct((m, d), F32),
        compiler_params=_cparams("arbitrary"),
    )(pos, y_sorted, x, gates, g.reshape(1, d), b.reshape(1, d))


def _route_tables(top_i, n_exp, tm):
    m = top_i.shape[0]
    n_rows = m * TOP_K
    n_tiles = -(-(n_rows + n_exp * (tm - 1)) // tm)
    e_flat = top_i.reshape(-1)
    order = jnp.argsort(e_flat, stable=True)
    counts = jnp.sum(e_flat[:, None] == jnp.arange(n_exp, dtype=jnp.int32)[None, :], axis=0).astype(jnp.int32)
    padded = (counts + tm - 1) // tm * tm
    pad_end = jnp.cumsum(padded)
    pad_start = pad_end - padded
    cnt_start = jnp.cumsum(counts) - counts
    e_sorted = e_flat[order]
    dest_sorted = pad_start[e_sorted] + jnp.arange(n_rows, dtype=jnp.int32) - cnt_start[e_sorted]
    pos = jnp.zeros((n_rows,), jnp.int32).at[order].set(dest_sorted)
    src = jnp.zeros((n_tiles * tm,), jnp.int32).at[dest_sorted].set((order // TOP_K).astype(jnp.int32))
    n_valid = pad_end[-1] // tm
    t_idx = jnp.arange(n_tiles, dtype=jnp.int32)
    t_eff = jnp.minimum(t_idx, jnp.maximum(n_valid - 1, 0))
    te = jnp.searchsorted(pad_end, t_eff * tm, side="right").astype(jnp.int32)
    te = jnp.minimum(te, n_exp - 1)
    tf = jnp.concatenate([jnp.ones((1,), jnp.int32), (te[1:] != te[:-1]).astype(jnp.int32)])
    tv = (t_idx < n_valid).astype(jnp.int32)
    return src, pos, (te, tf, tv, t_eff, t_idx)


def _rope(x, pos, hd):
    half = hd // 2
    inv = jnp.power(ROPE_THETA, -jnp.arange(half, dtype=F32) / half)
    ang = pos.astype(F32)[:, None] * inv[None, :]
    cos, sin = jnp.cos(ang)[:, None, :], jnp.sin(ang)[:, None, :]
    xh = x.reshape(x.shape[0], -1, hd)
    x1, x2 = xh[..., :half], xh[..., half:]
    return jnp.concatenate([x1 * cos - x2 * sin, x2 * cos + x1 * sin], axis=-1).reshape(x.shape)


def _even_mixer(x_bf, dims, cache_k, cache_v, state_conv, w_in, b_in, sinks, conv_w, conv_b, conv_g, conv_bn,
                w_out, b_out):
    nseq, seq, db, ds, n_p = dims
    m = x_bf.shape[0]
    _, w, n_kv, hd = cache_k.shape
    d_kv = n_kv * hd
    d_q = sinks.shape[0] * hd
    d_conv = state_conv.shape[-1]
    cw = conv_w.shape[0]
    tm = _tile(m, 640, BF16_SUBLANES)
    col_a, col_g = d_q + 2 * d_kv, d_q + 2 * d_kv + d_conv

    tm_wide = _tile(m, 416, BF16_SUBLANES)

    def proj(col, width):
        tn = _tile(math.gcd(col, width), 1024, LANES)
        return _dense(x_bf, w_in, tm=tm_wide if tn > 512 else tm, tn=tn, bias=b_in, col=col, n_out=width)

    pos = jnp.concatenate([jnp.tile(jnp.arange(seq, dtype=jnp.int32), nseq),
                           jnp.tile(PAST_LEN + jnp.arange(ds, dtype=jnp.int32), db)])
    q = _rope(proj(0, d_q), pos, hd).astype(BF16)
    if d_kv % LANES == 0:
        k, v = proj(d_q, d_kv), proj(d_q + d_kv, d_kv)
    else:
        kv = proj(d_q, col_a - d_q)
        k, v = kv[:, :d_kv], kv[:, d_kv:]
    k = _rope(k, pos, hd)
    u = _mm(x_bf, [w_in[None], w_in[None]], tm=tm, tn=_tile(math.gcd(col_a, col_g, d_conv), 256, LANES),
            cols=(col_a, col_g), n_out=d_conv, biases=[b_in, b_in], act="glu")
    k_bf, v_bf = k.astype(BF16), v.astype(BF16)

    att_p = _swa_prompt(q[:n_p], k_bf[:n_p], v_bf[:n_p], sinks, nseq=nseq, seq=seq, n_kv=n_kv, hd=hd, window=w)
    cache_k2 = cache_k.reshape(db, w, d_kv)
    cache_v2 = cache_v.reshape(db, w, d_kv)
    att_s = _swa_sample(q[n_p:].reshape(db, ds, d_q), k_bf[n_p:].reshape(db, ds, d_kv),
                        v_bf[n_p:].reshape(db, ds, d_kv), cache_k2, cache_v2, sinks, n_kv=n_kv, hd=hd)
    conv_p = _conv_prompt(u[:n_p], conv_w, conv_b, conv_g, conv_bn, nseq=nseq, seq=seq)
    up_s = jnp.concatenate([state_conv, u[n_p:].reshape(db, ds, d_conv)], axis=1)
    conv_s = _conv_sample(up_s, conv_w, conv_b, conv_g, conv_bn, ds=ds)

    mix = jnp.concatenate([jnp.concatenate([att_p, att_s.reshape(db * ds, d_q)], axis=0),
                           jnp.concatenate([conv_p, conv_s.reshape(db * ds, d_conv)], axis=0)], axis=1)
    y = _dense(mix, w_out, tm=tm_wide, tn=_tile(w_out.shape[1], 1024, LANES), bias=b_out)

    kp = k[:n_p].reshape(nseq, seq, n_kv, hd)[:, -w:]
    vp = v[:n_p].reshape(nseq, seq, n_kv, hd)[:, -w:]
    ks = jnp.concatenate([cache_k, k[n_p:].reshape(db, ds, n_kv, hd)], axis=1)[:, -w:]
    vs = jnp.concatenate([cache_v, v[n_p:].reshape(db, ds, n_kv, hd)], axis=1)[:, -w:]
    cp = u[:n_p].reshape(nseq, seq, d_conv)[:, -(cw - 1):]
    cs = up_s[:, -(cw - 1):]
    return y, (kp, vp, ks, vs, cp, cs)


def _rwkv_mixer(x, dims, state_shift, state_wkv, mu, w0, w1, w2, a0, a1, a2, g1, g2, k_k, k_a, r_k, w_r, w_k, w_v,
                w_o, lnx_g, lnx_b):
    nseq, seq, db, ds, n_p = dims
    m, d = x.shape
    n_heads, hc = state_wkv.shape[1], state_wkv.shape[-1]
    xp = x[:n_p].reshape(nseq, seq, d)
    xs = x[n_p:].reshape(db, ds, d)
    prev_p = jnp.concatenate([jnp.zeros((nseq, 1, d), F32), xp[:, :-1]], axis=1)
    prev_s = jnp.concatenate([state_shift[:, None], xs[:, :-1]], axis=1)
    prev = jnp.concatenate([prev_p.reshape(n_p, d), prev_s.reshape(db * ds, d)], axis=0)
    xx = prev - x
    xr, xw, xk, xv, xa, xg = [(x + xx * mu[i]).astype(BF16) for i in range(6)]
    tm = _tile(m, 416, BF16_SUBLANES)
    big = lambda z, wt: _dense(z, wt, tm=tm, tn=_tile(wt.shape[1], 1024, LANES))
    r = big(xr, w_r)
    k = big(xk, w_k)
    v = big(xv, w_v)
    lw = big(jnp.tanh(big(xw, w1)).astype(BF16), w2)
    la = big(big(xa, a1).astype(BF16), a2)
    g = big(jax.nn.sigmoid(big(xg, g1)).astype(BF16), g2)

    def part(rows, n_s, t, state, round_state):
        r_, k_, v_, lw_, la_, g_ = [z[rows] for z in (r, k, v, lw, la, g)]
        n = n_s * t
        heads = lambda z: z.reshape(n, n_heads, hc)
        w = -jax.nn.softplus(-(w0 + lw_)) - 0.5
        a = jax.nn.sigmoid(a0 + la_)
        kk = heads(k_ * k_k)
        kk = (kk * lax.rsqrt(jnp.maximum(jnp.sum(kk * kk, axis=-1, keepdims=True), 1e-24))).reshape(n, d)
        k_ = k_ * (1 + (a - 1) * k_a)
        decay = jnp.exp(-jnp.exp(w))
        wkv_in = [z.reshape(n_s, t, d) for z in (r_, decay, k_, v_, kk, kk * a)]
        o, st = _wkv(*wkv_in, state, hc=hc, round_state=round_state)
        o = heads(o.reshape(n, d))
        o_mean = jnp.mean(o, axis=-1, keepdims=True)
        o_var = jnp.mean(jnp.square(o - o_mean), axis=-1, keepdims=True)
        o = ((o - o_mean) * lax.rsqrt(o_var + GN_EPS)).reshape(n, d) * lnx_g + lnx_b
        bonus = jnp.sum(heads(r_) * heads(k_) * r_k, axis=-1, keepdims=True) * heads(v_)
        return ((o + bonus.reshape(n, d)) * g_).astype(BF16), st

    og_p, wkv_p = part(slice(0, n_p), nseq, seq, jnp.zeros((nseq, n_heads, hc, hc), F32), False)
    og_s, wkv_s = part(slice(n_p, m), db, ds, state_wkv, True)
    y = big(jnp.concatenate([og_p, og_s], axis=0), w_o)
    return y, (xp[:, -1], xs[:, -1], wkv_p, wkv_s)


def _swiglu_dense(x_bf, wg, wu, wd, *, tm):
    d_ff = wg.shape[1]
    h = _mm(x_bf, [wg[None], wu[None]], tm=_tile(x_bf.shape[0], 832, BF16_SUBLANES), tn=_tile(d_ff, 256, LANES),
            act="swiglu", out_dtype=BF16)
    nk = 2 if d_ff % (2 * LANES) == 0 and d_ff > 8192 else 1
    return _dense(h, wd, tm=tm, tn=_tile(wd.shape[1], 512, LANES), nk=nk)


def _moe(x, x_ln_args, w_router, wg, wu, wd, alpha):
    m, d = x.shape
    n_exp, _, d_ff = wg.shape
    tm = 512 if m * TOP_K >= 4096 else 64
    top_i, gates = _router(x, w_router)
    src, pos, tiles = _route_tables(top_i, n_exp, tm)
    xs = _gather_rows(x, src, tr=_tile(src.shape[0], 256, BF16_SUBLANES))
    h = _mm(xs, [wg, wu], tm=tm, tn=_tile(d_ff, 512, LANES), act="swiglu", out_dtype=BF16, tiles=tiles)
    nk = 4 if d_ff % (4 * LANES) == 0 and d_ff > 8192 else 1
    y = None
    for kk in range(nk):
        y = _mm(h, [wd], tm=tm, tn=_tile(d, 512, LANES), nk=nk, kk=kk, add=y, tiles=tiles)
    g, b = x_ln_args
    return _combine_ln(y, pos, gates, x, g, b, alpha)


def kernel(x_prompt, x_sample, cache_swa_k, cache_swa_v, state_conv, state_shift, state_wkv, ln_g, ln_b, w_in, b_in,
           sinks, conv_w, conv_b, conv_ln_g, conv_ln_b, w_out, b_out, ffn_w_gate, ffn_w_up, ffn_w_down, mu, w0, w1,
           w2, a0, a1, a2, g1, g2, k_k, k_a, r_k, w_r, w_k, w_v, w_o, lnx_g, lnx_b, w_router, moe_w_gate, moe_w_up,
           moe_w_down):
    nseq, seq, d = x_prompt.shape
    db, ds, _ = x_sample.shape
    n_p = nseq * seq
    dims = (nseq, seq, db, ds, n_p)
    depth = ln_g.shape[0]
    alpha = (2 * depth) ** 0.25
    x = jnp.concatenate([x_prompt.reshape(n_p, d), x_sample.reshape(db * ds, d)], axis=0)
    m = x.shape[0]
    tm = _tile(m, 640, BF16_SUBLANES)
    x_bf = x.astype(BF16)
    even_out, odd_out = [], []
    for layer in range(depth):
        i = layer // 2
        if layer % 2 == 0:
            y, states = _even_mixer(x_bf, dims, cache_swa_k[i], cache_swa_v[i], state_conv[i], w_in[i], b_in[i],
                                    sinks[i], conv_w[i], conv_b[i], conv_ln_g[i], conv_ln_b[i], w_out[i], b_out[i])
            even_out.append(states)
        else:
            y, states = _rwkv_mixer(x, dims, state_shift[i], state_wkv[i], mu[i], w0[i], w1[i], w2[i], a0[i], a1[i],
                                    a2[i], g1[i], g2[i], k_k[i], k_a[i], r_k[i], w_r[i], w_k[i], w_v[i], w_o[i],
                                    lnx_g[i], lnx_b[i])
            odd_out.append(states)
        x, x_bf = _add_ln(x, y, ln_g[layer, 0], ln_b[layer, 0], alpha)
        if layer % 2 == 0:
            f = _swiglu_dense(x_bf, ffn_w_gate[i], ffn_w_up[i], ffn_w_down[i], tm=tm)
            x, x_bf = _add_ln(x, f, ln_g[layer, 1], ln_b[layer, 1], alpha)
        else:
            x = _moe(x, (ln_g[layer, 1], ln_b[layer, 1]), w_router[i], moe_w_gate[i], moe_w_up[i], moe_w_down[i],
                     alpha)
            x_bf = x.astype(BF16)
    stack = lambda outs, j: jnp.stack([o[j] for o in outs])
    return (x[:n_p].reshape(nseq, seq, d), x[n_p:].reshape(db, ds, d),
            stack(even_out, 0), stack(even_out, 1), stack(even_out, 2), stack(even_out, 3),
            stack(even_out, 4), stack(even_out, 5),
            stack(odd_out, 0), stack(odd_out, 1), stack(odd_out, 2), stack(odd_out, 3))
```
